```python
import jax, jax.numpy as jnp
from jax import lax
import numpy as np

D_MODEL = 1024
BATCH = 16
SEQ = 256
DEPTH = 4
DEC_BATCH = 8
DEC_SEQ = 2048
PAST_LEN = 512

GRID_W = 64
N_MIXERS = 3
N_HEADS = 16
N_KV_HEADS = 4
HEAD_DIM = 64
Q_PER_KV = N_HEADS // N_KV_HEADS
QKV_DIM = (N_HEADS + 2 * N_KV_HEADS) * HEAD_DIM
WINDOW = 128
BLOCK = 128
ROPE_THETA = 10000.0
MLA_HEADS = 16
MLA_Q_LORA = 384
MLA_KV_LORA = 256
MLA_NOPE = 64
MLA_ROPE = 32
MLA_V = 64
N_EXPERTS = 16
N_GROUPS = 4
EXPERTS_PER_GROUP = N_EXPERTS // N_GROUPS
TOP_K = 2
D_EXPERT = 256
EPS = 1e-6
N_LAYERS_A = (DEPTH + 2) // 3
N_LAYERS_B = (DEPTH + 1) // 3
N_LAYERS_C = DEPTH // 3

kernel_name = 'hybrid_dit_prefix_denoise_step'


def rmsnorm(x, g):
    xf = x.astype(jnp.float32)
    y = xf * lax.rsqrt(jnp.mean(xf * xf, axis=-1, keepdims=True) + EPS)
    return (y * g.astype(jnp.float32)).astype(x.dtype)


def modulate(x, g, shift, scale):
    return rmsnorm(x, g) * (1 + scale) + shift


def adaln(cond, w_mod, b_mod):
    m = jax.nn.silu(cond) @ w_mod + b_mod
    return jnp.split(m, 6, axis=-1)


def grid_positions(n_tokens):
    rows = n_tokens // GRID_W
    r, col = jnp.meshgrid(jnp.arange(rows), jnp.arange(GRID_W), indexing='ij')
    return r.reshape(-1).astype(jnp.float32), col.reshape(-1).astype(jnp.float32)


def rope_2d(x, row, col):
    d = x.shape[-1]
    half, quarter = d // 2, d // 4
    inv = ROPE_THETA ** (-jnp.arange(quarter, dtype=jnp.float32) / quarter)
    ang = jnp.concatenate([row[:, None] * inv, col[:, None] * inv], axis=-1)
    cos = jnp.cos(ang)[:, None, :].astype(x.dtype)
    sin = jnp.sin(ang)[:, None, :].astype(x.dtype)
    x1, x2 = x[..., :half], x[..., half:]
    return jnp.concatenate([x1 * cos - x2 * sin, x2 * cos + x1 * sin], axis=-1)


def attend(qb, k, v, mask=None, sink=None):
    scale = qb.shape[-1] ** -0.5
    s = jnp.einsum('bqkgd,bskd->bkgqs', qb, k).astype(jnp.float32) * scale
    if mask is not None:
        s = jnp.where(mask, s, -jnp.inf)
    m = jnp.max(s, axis=-1, keepdims=True)
    if sink is not None:
        sk = sink.astype(jnp.float32)[None, :, :, None, None]
        m = jnp.maximum(m, sk)
        e = jnp.exp(s - m)
        denom = jnp.sum(e, axis=-1, keepdims=True) + jnp.exp(sk - m)
    else:
        e = jnp.exp(s - m)
        denom = jnp.sum(e, axis=-1, keepdims=True)
    p = (e / denom).astype(v.dtype)
    return jnp.einsum('bkgqs,bskd->bqkgd', p, v)


def sweep_query_blocks(q, fn):
    b, s = q.shape[:2]
    nb = s // BLOCK
    qb = jnp.moveaxis(q.reshape(b, nb, BLOCK, *q.shape[2:]), 1, 0)
    out = lax.map(lambda args: fn(*args), (qb, jnp.arange(nb)))
    return jnp.moveaxis(out, 0, 1).reshape(b, s, *out.shape[3:])


def band_mask(n, s_lat, s_ctx):
    q_pos = n * BLOCK + jnp.arange(BLOCK)
    k_pos = (n - 1) * BLOCK + jnp.arange(3 * BLOCK)
    near = (jnp.abs(k_pos[None, :] - q_pos[:, None]) <= WINDOW) & (k_pos >= 0)[None, :] & (k_pos < s_lat)[None, :]
    return jnp.concatenate([near, jnp.ones((BLOCK, s_ctx), dtype=bool)], axis=1)


def gqa_project(h, wqkv):
    b, s, _ = h.shape
    q, k, v = jnp.split(h @ wqkv, [N_HEADS * HEAD_DIM, (N_HEADS + N_KV_HEADS) * HEAD_DIM], axis=-1)
    return (q.reshape(b, s, N_HEADS, HEAD_DIM), k.reshape(b, s, N_KV_HEADS, HEAD_DIM),
            v.reshape(b, s, N_KV_HEADS, HEAD_DIM))


def group_q(q):
    return q.reshape(q.shape[0], q.shape[1], N_KV_HEADS, Q_PER_KV, q.shape[-1])


def merge_heads(o, wo):
    return o.reshape(o.shape[0], o.shape[1], -1) @ wo


def swa_context(h, wqkv, wo, sink):
    q, k, v = gqa_project(h, wqkv)
    sink_g = sink.reshape(N_KV_HEADS, Q_PER_KV)
    o = sweep_query_blocks(group_q(q), lambda qb, n: attend(qb, k, v, None, sink_g))
    return merge_heads(o, wo), k, v


def swa_latent(h, wqkv, wo, sink, k_ctx, v_ctx, row, col):
    q, k, v = gqa_project(h, wqkv)
    q, k = rope_2d(q, row, col), rope_2d(k, row, col)
    s_lat, s_ctx = h.shape[1], k_ctx.shape[1]
    pad = ((0, 0), (BLOCK, BLOCK), (0, 0), (0, 0))
    k_pad, v_pad = jnp.pad(k, pad), jnp.pad(v, pad)
    sink_g = sink.reshape(N_KV_HEADS, Q_PER_KV)

    def block(qb, n):
        k_loc = lax.dynamic_slice_in_dim(k_pad, n * BLOCK, 3 * BLOCK, axis=1)
        v_loc = lax.dynamic_slice_in_dim(v_pad, n * BLOCK, 3 * BLOCK, axis=1)
        keys = jnp.concatenate([k_loc, k_ctx.astype(k_loc.dtype)], axis=1)
        vals = jnp.concatenate([v_loc, v_ctx.astype(v_loc.dtype)], axis=1)
        return attend(qb, keys, vals, band_mask(n, s_lat, s_ctx), sink_g)

    return merge_heads(sweep_query_blocks(group_q(q), block), wo)


def qkn_context(h, wqkv, wo, qn, kn):
    q, k, v = gqa_project(h, wqkv)
    q, k = rmsnorm(q, qn), rmsnorm(k, kn)
    o = sweep_query_blocks(group_q(q), lambda qb, n: attend(qb, k, v))
    return merge_heads(o, wo), k, v


def qkn_latent(h, wqkv, wo, qn, kn, k_ctx, v_ctx, row, col):
    q, k, v = gqa_project(h, wqkv)
    q = rope_2d(rmsnorm(q, qn), row, col)
    k = rope_2d(rmsnorm(k, kn), row, col)
    keys = jnp.concatenate([k, k_ctx.astype(k.dtype)], axis=1)
    vals = jnp.concatenate([v, v_ctx.astype(v.dtype)], axis=1)
    o = sweep_query_blocks(group_q(q), lambda qb, n: attend(qb, keys, vals))
    return merge_heads(o, wo)


def mla_queries(h, wdq, qnorm, wuq):
    b, s, _ = h.shape
    q = (rmsnorm(h @ wdq, qnorm) @ wuq).reshape(b, s, MLA_HEADS, MLA_NOPE + MLA_ROPE)
    return q[..., :MLA_NOPE], q[..., MLA_NOPE:]


def mla_compress(h, wdkv, kvnorm):
    ckv = h @ wdkv
    return rmsnorm(ckv[..., :MLA_KV_LORA], kvnorm), ckv[..., MLA_KV_LORA:]


def mla_expand(ckv, k_pe, wukv):
    b, s, _ = ckv.shape
    kv = (ckv @ wukv).reshape(b, s, MLA_HEADS, MLA_NOPE + MLA_V)
    k_rot = jnp.broadcast_to(k_pe[:, :, None, :], (b, s, MLA_HEADS, MLA_ROPE))
    return jnp.concatenate([kv[..., :MLA_NOPE], k_rot], axis=-1), kv[..., MLA_NOPE:]


def mla_context(h, wdq, qnorm, wuq, wdkv, kvnorm, wukv, wo):
    q_nope, q_pe = mla_queries(h, wdq, qnorm, wuq)
    ckv, k_pe = mla_compress(h, wdkv, kvnorm)
    k, v = mla_expand(ckv, k_pe, wukv)
    q = jnp.concatenate([q_nope, q_pe], axis=-1)[:, :, :, None, :]
    o = sweep_query_blocks(q, lambda qb, n: attend(qb, k, v))
    return merge_heads(o, wo), ckv, k_pe


def mla_latent(h, wdq, qnorm, wuq, wdkv, kvnorm, wukv, wo, ckv_ctx, kpe_ctx, row, col):
    q_nope, q_pe = mla_queries(h, wdq, qnorm, wuq)
    q_pe = rope_2d(q_pe, row, col)
    ckv, k_pe = mla_compress(h, wdkv, kvnorm)
    k_pe = rope_2d(k_pe[:, :, None, :], row, col)[:, :, 0, :]
    ckv_all = jnp.concatenate([ckv, ckv_ctx.astype(ckv.dtype)], axis=1)
    kpe_all = jnp.concatenate([k_pe, kpe_ctx.astype(k_pe.dtype)], axis=1)
    k, v = mla_expand(ckv_all, kpe_all, wukv)
    q = jnp.concatenate([q_nope, q_pe], axis=-1)[:, :, :, None, :]
    o = sweep_query_blocks(q, lambda qb, n: attend(qb, k, v))
    return merge_heads(o, wo)


def moe(h, w_router, router_bias, w_gate, w_up, w_down):
    s = jax.nn.sigmoid((h @ w_router).astype(jnp.float32))
    sb = s + router_bias.astype(jnp.float32)
    grouped = sb.reshape(*sb.shape[:-1], N_GROUPS, EXPERTS_PER_GROUP)
    group_score = jnp.sum(lax.top_k(grouped, 2)[0], axis=-1)
    gsel = jnp.argmax(group_score, axis=-1)
    gmask = jnp.arange(N_GROUPS) == gsel[..., None]
    emask = jnp.repeat(gmask, EXPERTS_PER_GROUP, axis=-1)
    _, idx = lax.top_k(jnp.where(emask, sb, -jnp.inf), TOP_K)
    w = jnp.take_along_axis(s, idx, axis=-1)
    w = w / jnp.sum(w, axis=-1, keepdims=True)
    gates = jnp.sum(jax.nn.one_hot(idx, N_EXPERTS, dtype=jnp.float32) * w[..., None], axis=-2)
    a = jnp.einsum('bsd,edf->bsef', h, w_gate)
    u = jnp.einsum('bsd,edf->bsef', h, w_up)
    act = jax.nn.silu(a) * u * gates[..., None].astype(h.dtype)
    return jnp.einsum('bsef,efd->bsd', act, w_down)


def setup_inputs(seed: int = 0) -> dict:
    key = jax.random.key(seed)
    ks = iter(jax.random.split(key, 48))

    def nrm(shape, scale):
        return scale * jax.random.normal(next(ks), shape, jnp.float32)

    def gain(shape):
        return 1.0 + 0.05 * jax.random.normal(next(ks), shape, jnp.float32)

    D = D_MODEL
    return {
        'x_prompt': nrm((BATCH, SEQ, D), 1.0),
        'x_sample': nrm((DEC_BATCH, DEC_SEQ, D), 1.0),
        'cache_swa_k': nrm((DEC_BATCH, N_LAYERS_A, PAST_LEN, N_KV_HEADS, HEAD_DIM), 1.0),
        'cache_swa_v': nrm((DEC_BATCH, N_LAYERS_A, PAST_LEN, N_KV_HEADS, HEAD_DIM), 1.0),
        'cache_qkn_k': nrm((DEC_BATCH, N_LAYERS_B, PAST_LEN, N_KV_HEADS, HEAD_DIM), 1.0),
        'cache_qkn_v': nrm((DEC_BATCH, N_LAYERS_B, PAST_LEN, N_KV_HEADS, HEAD_DIM), 1.0),
        'cache_mla_ckv': nrm((DEC_BATCH, N_LAYERS_C, PAST_LEN, MLA_KV_LORA), 1.0),
        'cache_mla_kpe': nrm((DEC_BATCH, N_LAYERS_C, PAST_LEN, MLA_ROPE), 1.0),
        'c': nrm((DEC_BATCH, D), 1.0),
        'c_ctx': nrm((D,), 1.0),
        'w_mod': nrm((DEPTH, D, 6 * D), 0.3 * D ** -0.5),
        'b_mod': nrm((DEPTH, 6 * D), 0.02),
        'norm_mix': gain((DEPTH, D)),
        'norm_ffn': gain((DEPTH, D)),
        'swa_wqkv': nrm((N_LAYERS_A, D, QKV_DIM), D ** -0.5),
        'swa_wo': nrm((N_LAYERS_A, N_HEADS * HEAD_DIM, D), (N_HEADS * HEAD_DIM) ** -0.5),
        'swa_sink': nrm((N_LAYERS_A, N_HEADS), 1.0),
        'qkn_wqkv': nrm((N_LAYERS_B, D, QKV_DIM), D ** -0.5),
        'qkn_wo': nrm((N_LAYERS_B, N_HEADS * HEAD_DIM, D), (N_HEADS * HEAD_DIM) ** -0.5),
        'qkn_qnorm': gain((N_LAYERS_B, HEAD_DIM)),
        'qkn_knorm': gain((N_LAYERS_B, HEAD_DIM)),
        'mla_wdq': nrm((N_LAYERS_C, D, MLA_Q_LORA), D ** -0.5),
        'mla_qnorm': gain((N_LAYERS_C, MLA_Q_LORA)),
        'mla_wuq': nrm((N_LAYERS_C, MLA_Q_LORA, MLA_HEADS * (MLA_NOPE + MLA_ROPE)), MLA_Q_LORA ** -0.5),
        'mla_wdkv': nrm((N_LAYERS_C, D, MLA_KV_LORA + MLA_ROPE), D ** -0.5),
        'mla_kvnorm': gain((N_LAYERS_C, MLA_KV_LORA)),
        'mla_wukv': nrm((N_LAYERS_C, MLA_KV_LORA, MLA_HEADS * (MLA_NOPE + MLA_V)), MLA_KV_LORA ** -0.5),
        'mla_wo': nrm((N_LAYERS_C, MLA_HEADS * MLA_V, D), (MLA_HEADS * MLA_V) ** -0.5),
        'w_router': nrm((D, N_EXPERTS), D ** -0.5),
        'router_bias': nrm((N_EXPERTS,), 0.01),
        'moe_w_gate': nrm((DEPTH, N_EXPERTS, D, D_EXPERT), D ** -0.5),
        'moe_w_up': nrm((DEPTH, N_EXPERTS, D, D_EXPERT), D ** -0.5),
        'moe_w_down': nrm((DEPTH, N_EXPERTS, D_EXPERT, D), D_EXPERT ** -0.5),
        'final_norm': gain((D,)),
    }


def reference(x_prompt, x_sample, cache_swa_k, cache_swa_v, cache_qkn_k, cache_qkn_v, cache_mla_ckv,
              cache_mla_kpe, c, c_ctx, w_mod, b_mod, norm_mix, norm_ffn, swa_wqkv, swa_wo, swa_sink,
              qkn_wqkv, qkn_wo, qkn_qnorm, qkn_knorm, mla_wdq, mla_qnorm, mla_wuq, mla_wdkv, mla_kvnorm,
              mla_wukv, mla_wo, w_router, router_bias, moe_w_gate, moe_w_up, moe_w_down, final_norm):
    xc, xl = x_prompt, x_sample
    row, col = grid_positions(x_sample.shape[1])
    swa_k, swa_v, qkn_k, qkn_v, mla_ckv, mla_kpe = [], [], [], [], [], []
    for i in range(DEPTH):
        kind, j = i % N_MIXERS, i // N_MIXERS
        sh1c, sc1c, g1c, sh2c, sc2c, g2c = adaln(c_ctx, w_mod[i], b_mod[i])
        sh1l, sc1l, g1l, sh2l, sc2l, g2l = [m[:, None, :] for m in adaln(c, w_mod[i], b_mod[i])]
        hc = modulate(xc, norm_mix[i], sh1c, sc1c)
        hl = modulate(xl, norm_mix[i], sh1l, sc1l)
        if kind == 0:
            oc, kc, vc = swa_context(hc, swa_wqkv[j], swa_wo[j], swa_sink[j])
            ol = swa_latent(hl, swa_wqkv[j], swa_wo[j], swa_sink[j], cache_swa_k[:, j], cache_swa_v[:, j], row, col)
            swa_k.append(kc)
            swa_v.append(vc)
        elif kind == 1:
            oc, kc, vc = qkn_context(hc, qkn_wqkv[j], qkn_wo[j], qkn_qnorm[j], qkn_knorm[j])
            ol = qkn_latent(hl, qkn_wqkv[j], qkn_wo[j], qkn_qnorm[j], qkn_knorm[j],
                            cache_qkn_k[:, j], cache_qkn_v[:, j], row, col)
            qkn_k.append(kc)
            qkn_v.append(vc)
        else:
            oc, ckv_c, kpe_c = mla_context(hc, mla_wdq[j], mla_qnorm[j], mla_wuq[j], mla_wdkv[j],
                                           mla_kvnorm[j], mla_wukv[j], mla_wo[j])
            ol = mla_latent(hl, mla_wdq[j], mla_qnorm[j], mla_wuq[j], mla_wdkv[j], mla_kvnorm[j],
                            mla_wukv[j], mla_wo[j], cache_mla_ckv[:, j], cache_mla_kpe[:, j], row, col)
            mla_ckv.append(ckv_c)
            mla_kpe.append(kpe_c)
        xc = xc + g1c * oc
        xl = xl + g1l * ol
        hc = modulate(xc, norm_ffn[i], sh2c, sc2c)
        hl = modulate(xl, norm_ffn[i], sh2l, sc2l)
        xc = xc + g2c * moe(hc, w_router, router_bias, moe_w_gate[i], moe_w_up[i], moe_w_down[i])
        xl = xl + g2l * moe(hl, w_router, router_bias, moe_w_gate[i], moe_w_up[i], moe_w_down[i])
    y_prompt = rmsnorm(xc, final_norm)
    y_sample = rmsnorm(xl, final_norm)
    state_swa_k = jnp.stack(swa_k, axis=1)
    state_swa_v = jnp.stack(swa_v, axis=1)
    state_qkn_k = jnp.stack(qkn_k, axis=1)
    state_qkn_v = jnp.stack(qkn_v, axis=1)
    state_mla_ckv = jnp.stack(mla_ckv, axis=1)
    state_mla_kpe = jnp.stack(mla_kpe, axis=1)
    return (y_prompt, y_sample, state_swa_k, state_swa_v, state_qkn_k, state_qkn_v, state_mla_ckv, state_mla_kpe)
```

```python
import functools

import numpy as np
import jax
import jax.numpy as jnp
from jax import lax
from jax.experimental import pallas as pl
from jax.experimental.pallas import tpu as pltpu

F32 = jnp.float32
BF16 = jnp.bfloat16

D_MODEL = 1024
DEPTH = 4
GRID_W = 64
N_MIXERS = 3
N_HEADS = 16
N_KV_HEADS = 4
HEAD_DIM = 64
QKV_DIM = (N_HEADS + 2 * N_KV_HEADS) * HEAD_DIM
WINDOW = 128
ROPE_THETA = 10000.0
MLA_HEADS = 16
MLA_Q_LORA = 384
MLA_KV_LORA = 256
MLA_NOPE = 64
MLA_ROPE = 32
MLA_V = 64
N_EXPERTS = 16
N_GROUPS = 4
EXPERTS_PER_GROUP = N_EXPERTS // N_GROUPS
D_EXPERT = 256
EPS = 1e-6

LANES = 128
COND_ROWS = 16
MLA_DOWN = 768
MLA_CK = 384
NEG_BIG = -1e30
VMEM_LIMIT = 56 * 1024 * 1024


def _cparams(sem):
    return pltpu.CompilerParams(dimension_semantics=sem, vmem_limit_bytes=VMEM_LIMIT)


def _sigmoid(x):
    return 1.0 / (1.0 + jnp.exp(-x))


def _rms(x, g):
    return x * lax.rsqrt(jnp.mean(x * x, axis=-1, keepdims=True) + EPS) * g


def _dot(a, b):
    return jnp.dot(a, b, preferred_element_type=F32)


def _dot_nt(a, b):
    return lax.dot_general(a, b, (((1,), (1,)), ((), ())), preferred_element_type=F32)


def _adaln_kernel(cond_ref, w_ref, b_ref, o_ref):
    c = cond_ref[...]
    s = (c * _sigmoid(c)).astype(BF16)
    o_ref[...] = _dot(s, w_ref[...].astype(BF16)) + b_ref[...]


def _adaln(cond, w_mod, b_mod):
    tn = 1536
    n = 6 * D_MODEL
    out = pl.pallas_call(
        _adaln_kernel,
        grid=(DEPTH, n // tn),
        in_specs=[
            pl.BlockSpec((COND_ROWS, D_MODEL), lambda l, j: (0, 0)),
            pl.BlockSpec((None, D_MODEL, tn), lambda l, j: (l, 0, j)),
            pl.BlockSpec((None, 1, tn), lambda l, j: (l, 0, j)),
        ],
        out_specs=pl.BlockSpec((None, COND_ROWS, tn), lambda l, j: (l, 0, j)),
        out_shape=jax.ShapeDtypeStruct((DEPTH, COND_ROWS, n), F32),
        compiler_params=_cparams(("parallel", "parallel")),
        name="adaln",
    )(cond, w_mod, b_mod.reshape(DEPTH, 1, n))
    return out.reshape(DEPTH, COND_ROWS, 6, D_MODEL)


def _mod_spec(rows_per_cond):
    if rows_per_cond == 0:
        return pl.BlockSpec((1, 6, D_MODEL), lambda i: (0, 0, 0))
    return pl.BlockSpec((1, 6, D_MODEL), lambda i: (1 + i // rows_per_cond, 0, 0))


def _rope_tables(seq, head_dim, lane0, period):
    half, quarter = head_dim // 2, head_dim // 4
    pos = np.arange(seq)
    row, col = (pos // GRID_W).astype(np.float64), (pos % GRID_W).astype(np.float64)
    inv = ROPE_THETA ** (-np.arange(quarter, dtype=np.float64) / quarter)
    ang = np.concatenate([row[:, None] * inv, col[:, None] * inv], axis=-1)
    cos = np.ones((seq, LANES))
    s1 = np.zeros((seq, LANES))
    s2 = np.zeros((seq, LANES))
    starts = [lane0] if period == 0 else list(range(lane0, LANES, period))
    for st in starts:
        cos[:, st:st + half] = np.cos(ang)
        cos[:, st + half:st + head_dim] = np.cos(ang)
        s1[:, st:st + half] = -np.sin(ang)
        s2[:, st + half:st + head_dim] = np.sin(ang)
    return tuple(jnp.asarray(t, F32) for t in (cos, s1, s2))


def _rope_chunk(x, cos, s1, s2, half):
    return x * cos + pltpu.roll(x, LANES - half, 1) * s1 + pltpu.roll(x, half, 1) * s2


def _head_layout(kv):
    lo = lax.broadcasted_iota(jnp.int32, (kv.shape[0], LANES), 1) < HEAD_DIM
    zero = jnp.zeros((kv.shape[0], LANES), F32)
    out = []
    for j in range(2):
        ch = kv[:, LANES * j:LANES * (j + 1)]
        ro = pltpu.roll(ch, HEAD_DIM, 1)
        out += [jnp.where(lo, ch, zero), jnp.where(lo, zero, ro), jnp.where(lo, ro, zero), jnp.where(lo, zero, ch)]
    return jnp.concatenate(out, axis=1)


def _proj_gqa_kernel(*refs, qknorm, rope, state):
    it = iter(refs)
    x_ref, mod_ref, g_ref, w_ref = next(it), next(it), next(it), next(it)
    if qknorm:
        qn_ref, kn_ref, gm_ref = next(it), next(it), next(it)
    if rope:
        cos_ref, s1_ref, s2_ref = next(it), next(it), next(it)
    q_ref, kp_ref, vp_ref = next(it), next(it), next(it)
    if state:
        k32_ref, v32_ref = next(it), next(it)

    mod = mod_ref[0]
    h = _rms(x_ref[...], g_ref[...]) * (1.0 + mod[1:2]) + mod[0:1]
    qkv = _dot(h.astype(BF16), w_ref[...])
    nq = N_HEADS * HEAD_DIM
    nk = N_KV_HEADS * HEAD_DIM
    chunks = [qkv[:, LANES * j:LANES * (j + 1)] for j in range((nq + nk) // LANES)]
    if qknorm:
        gains = [qn_ref[...]] * (nq // LANES) + [kn_ref[...]] * (nk // LANES)
        gm = gm_ref[...]
        chunks = [c * lax.rsqrt(_dot((c * c).astype(BF16), gm) + EPS) * g for c, g in zip(chunks, gains)]
    k_state = jnp.concatenate(chunks[nq // LANES:], axis=1)
    if rope:
        cos, s1, s2 = cos_ref[...], s1_ref[...], s2_ref[...]
        chunks = [_rope_chunk(c, cos, s1, s2, HEAD_DIM // 2) for c in chunks]
    q = jnp.concatenate(chunks[:nq // LANES], axis=1) * (HEAD_DIM ** -0.5)
    k = jnp.concatenate(chunks[nq // LANES:], axis=1)
    v = qkv[:, nq + nk:]
    q_ref[...] = q.astype(BF16)
    kp_ref[...] = _head_layout(k).astype(BF16)
    vp_ref[...] = _head_layout(v).astype(BF16)
    if state:
        k32_ref[...] = k_state
        v32_ref[...] = v


def _proj_gqa(x, mods, gain, w, qn, kn, tables, *, latent, tm):
    n = x.shape[0]
    qknorm = qn is not None
    tiles_per_seq = 2048 // tm
    ins = [x, mods, gain.reshape(1, D_MODEL), w]
    specs = [
        pl.BlockSpec((tm, D_MODEL), lambda i: (i, 0)),
        _mod_spec(tiles_per_seq if latent else 0),
        pl.BlockSpec((1, D_MODEL), lambda i: (0, 0)),
        pl.BlockSpec((D_MODEL, QKV_DIM), lambda i: (0, 0)),
    ]
    if qknorm:
        gm = np.kron(np.eye(2), np.full((HEAD_DIM, HEAD_DIM), 1.0 / HEAD_DIM))
        ins += [jnp.tile(qn, 2).reshape(1, LANES), jnp.tile(kn, 2).reshape(1, LANES), jnp.asarray(gm, BF16)]
        specs += [pl.BlockSpec((1, LANES), lambda i: (0, 0))] * 2 + [pl.BlockSpec((LANES, LANES), lambda i: (0, 0))]
    if latent:
        ins += list(tables)
        specs += [pl.BlockSpec((tm, LANES), lambda i: (i % tiles_per_seq, 0))] * 3
    wide = N_HEADS * HEAD_DIM
    outs = [jax.ShapeDtypeStruct((n, wide), BF16)] * 3
    ospecs = [pl.BlockSpec((tm, wide), lambda i: (i, 0))] * 3
    if not latent:
        nk = N_KV_HEADS * HEAD_DIM
        outs += [jax.ShapeDtypeStruct((n, nk), F32)] * 2
        ospecs += [pl.BlockSpec((tm, nk), lambda i: (i, 0))] * 2
    return pl.pallas_call(
        functools.partial(_proj_gqa_kernel, qknorm=qknorm, rope=latent, state=not latent),
        grid=(n // tm,),
        in_specs=specs,
        out_specs=ospecs,
        out_shape=outs,
        compiler_params=_cparams(("parallel",)),
        name="proj_gqa_lat" if latent else "proj_gqa_ctx",
    )(*ins)


def _proj_mla_kernel(*refs, rope, state):
    it = iter(refs)
    x_ref, mod_ref, g_ref, wd_ref, qn_ref, wuq_ref, kvn_ref = (next(it) for _ in range(7))
    if rope:
        qt = [next(it) for _ in range(3)]
        kt = [next(it) for _ in range(3)]
    q_ref, ck_ref = next(it), next(it)
    if state:
        ckv_ref, kpe_ref = next(it), next(it)

    mod = mod_ref[0]
    h = _rms(x_ref[...], g_ref[...]) * (1.0 + mod[1:2]) + mod[0:1]
    d = _dot(h.astype(BF16), wd_ref[...])
    cq = _rms(d[:, :MLA_Q_LORA], qn_ref[...])
    q = _dot(cq.astype(BF16), wuq_ref[...])
    ckv = _rms(d[:, MLA_Q_LORA:MLA_Q_LORA + MLA_KV_LORA], kvn_ref[...])
    kpe = d[:, MLA_Q_LORA + MLA_KV_LORA:]
    kpe_state = kpe
    scale = (MLA_NOPE + MLA_ROPE) ** -0.5
    qs = [q[:, LANES * j:LANES * (j + 1)] for j in range(MLA_HEADS)]
    if rope:
        cos, s1, s2 = (t[...] for t in qt)
        qs = [_rope_chunk(c, cos, s1, s2, MLA_ROPE // 2) for c in qs]
        kpe = _rope_chunk(kpe, *(t[...] for t in kt), MLA_ROPE // 2)
    q_ref[...] = (jnp.concatenate(qs, axis=1) * scale).astype(BF16)
    ck_ref[...] = jnp.concatenate([ckv, kpe], axis=1).astype(BF16)
    if state:
        ckv_ref[...] = ckv
        kpe_ref[...] = kpe_state[:, :MLA_ROPE]


def _proj_mla(x, mods, gain, wd, qn, wuq, kvn, qtables, ktables, *, latent, tm):
    n = x.shape[0]
    tiles_per_seq = 2048 // tm
    wq = MLA_HEADS * LANES
    ins = [x, mods, gain.reshape(1, D_MODEL), wd, qn.reshape(1, MLA_Q_LORA), wuq, kvn.reshape(1, MLA_KV_LORA)]
    specs = [
        pl.BlockSpec((tm, D_MODEL), lambda i: (i, 0)),
        _mod_spec(tiles_per_seq if latent else 0),
        pl.BlockSpec((1, D_MODEL), lambda i: (0, 0)),
        pl.BlockSpec((D_MODEL, MLA_DOWN), lambda i: (0, 0)),
        pl.BlockSpec((1, MLA_Q_LORA), lambda i: (0, 0)),
        pl.BlockSpec((MLA_Q_LORA, wq), lambda i: (0, 0)),
        pl.BlockSpec((1, MLA_KV_LORA), lambda i: (0, 0)),
    ]
    if latent:
        ins += list(qtables) + list(ktables)
        specs += [pl.BlockSpec((tm, LANES), lambda i: (i % tiles_per_seq, 0))] * 6
    outs = [jax.ShapeDtypeStruct((n, wq), BF16), jax.ShapeDtypeStruct((n, MLA_CK), BF16)]
    ospecs = [pl.BlockSpec((tm, wq), lambda i: (i, 0)), pl.BlockSpec((tm, MLA_CK), lambda i: (i, 0))]
    if not latent:
        outs += [jax.ShapeDtypeStruct((n, MLA_KV_LORA), F32), jax.ShapeDtypeStruct((n, MLA_ROPE), F32)]
        ospecs += [pl.BlockSpec((tm, MLA_KV_LORA), lambda i: (i, 0)), pl.BlockSpec((tm, MLA_ROPE), lambda i: (i, 0))]
    return pl.pallas_call(
        functools.partial(_proj_mla_kernel, rope=latent, state=not latent),
        grid=(n // tm,),
        in_specs=specs,
        out_specs=ospecs,
        out_shape=outs,
        compiler_params=_cparams(("parallel",)),
        name="proj_mla_lat" if latent else "proj_mla_ctx",
    )(*ins)


def _kv_expand_kernel(ck_ref, wk_ref, wv_ref, k_ref, v_ref):
    ck = ck_ref[...]
    k_ref[...] = _dot(ck, wk_ref[...]).astype(BF16)
    v_ref[...] = _dot(ck, wv_ref[...]).astype(BF16)


def _kv_expand(ck, wk, wv, tm):
    n = ck.shape[0]
    wide = MLA_HEADS * LANES
    return pl.pallas_call(
        _kv_expand_kernel,
        grid=(n // tm,),
        in_specs=[
            pl.BlockSpec((tm, MLA_CK), lambda i: (i, 0)),
            pl.BlockSpec((MLA_CK, wide), lambda i: (0, 0)),
            pl.BlockSpec((MLA_CK, wide), lambda i: (0, 0)),
        ],
        out_specs=[pl.BlockSpec((tm, wide), lambda i: (i, 0))] * 2,
        out_shape=[jax.ShapeDtypeStruct((n, wide), BF16)] * 2,
        compiler_params=_cparams(("parallel",)),
        name="mla_kv_expand",
    )(ck, wk, wv)


def _attn_kernel(*refs, pairs, n_seg, tq, band, use_sink, heads_per_step):
    it = iter(refs)
    if use_sink:
        sink_ref = next(it)
    q_ref = next(it)
    kv_refs = [(next(it), next(it)) for _ in range(n_seg)]
    o_ref = next(it)

    if band:
        qi = pl.program_id(2)
        seq = kv_refs[0][0].shape[0]
        win = tq + 2 * WINDOW
        start = pl.multiple_of(jnp.clip(qi * tq - WINDOW, 0, seq - win), WINDOW)
        q_pos = qi * tq + lax.broadcasted_iota(jnp.int32, (tq, win), 0)
        k_pos = start + lax.broadcasted_iota(jnp.int32, (tq, win), 1)
        near = jnp.abs(k_pos - q_pos) <= WINDOW

    for pi, (qe, qo, ke, ko, oo) in enumerate(pairs):
        acc = None
        for parity, (qoff, koff) in enumerate(((qe, ke), (qo, ko))):
            qh = q_ref[:, qoff:qoff + LANES]
            scores, values = [], []
            for si, (k_ref, v_ref) in enumerate(kv_refs):
                if band and si == 0:
                    kh = k_ref[pl.ds(start, win), koff:koff + LANES]
                    vh = v_ref[pl.ds(start, win), koff:koff + LANES]
                    s = jnp.where(near, _dot_nt(qh, kh), NEG_BIG)
                else:
                    kh = k_ref[:, koff:koff + LANES]
                    vh = v_ref[:, koff:koff + LANES]
                    s = _dot_nt(qh, kh)
                scores.append(s)
                values.append(vh)
            m = functools.reduce(jnp.maximum, [jnp.max(s, axis=-1, keepdims=True) for s in scores])
            if use_sink:
                sk = sink_ref[pl.program_id(1) * heads_per_step + 2 * pi + parity]
                m = jnp.maximum(m, sk)
            es = [jnp.exp(s - m) for s in scores]
            denom = functools.reduce(jnp.add, [jnp.sum(e, axis=-1, keepdims=True) for e in es])
            if use_sink:
                denom = denom + jnp.exp(sk - m)
            pv = functools.reduce(jnp.add, [_dot(e.astype(BF16), vh) for e, vh in zip(es, values)])
            pv = pv / denom
            acc = pv if acc is None else acc + pv
        o_ref[:, oo:oo + LANES] = acc.astype(o_ref.dtype)


def _attention(q, segs, *, gqa, steps, tq, band=False, sink=None):
    b, sq, wq_total = q.shape
    wk_total = segs[0][0].shape[2]
    wo_total = N_HEADS * HEAD_DIM
    wq, wk, wo = wq_total // steps, wk_total // steps, wo_total // steps
    n_pairs = wo // LANES
    if gqa:
        pairs = [(LANES * p, LANES * p, 2 * LANES * (p // 2), 2 * LANES * (p // 2) + LANES, LANES * p)
                 for p in range(n_pairs)]
    else:
        pairs = [(2 * LANES * p, 2 * LANES * p + LANES, 2 * LANES * p, 2 * LANES * p + LANES, LANES * p)
                 for p in range(n_pairs)]
    ins, specs = [], []
    if sink is not None:
        ins.append(sink)
        specs.append(pl.BlockSpec(memory_space=pltpu.SMEM))
    ins.append(q)
    specs.append(pl.BlockSpec((None, tq, wq), lambda bi, si, qi: (bi, qi, si)))
    for k, v in segs:
        sk = k.shape[1]
        ins += [k, v]
        specs += [pl.BlockSpec((None, sk, wk), lambda bi, si, qi: (bi, 0, si))] * 2
    return pl.pallas_call(
        functools.partial(_attn_kernel, pairs=pairs, n_seg=len(segs), tq=tq, band=band,
                          use_sink=sink is not None, heads_per_step=2 * n_pairs),
        grid=(b, steps, sq // tq),
        in_specs=specs,
        out_specs=pl.BlockSpec((None, tq, wo), lambda bi, si, qi: (bi, qi, si)),
        out_shape=jax.ShapeDtypeStruct((b, sq, wo_total), BF16),
        compiler_params=_cparams(("parallel", "parallel", "parallel")),
        name="attention",
    )(*ins)


def _route(logits_t, bias_col):
    s = _sigmoid(logits_t)
    sb = s + bias_col
    r = [sb[e:e + 1, :] for e in range(N_EXPERTS)]
    sr = [s[e:e + 1, :] for e in range(N_EXPERTS)]
    gscore = []
    for g in range(N_GROUPS):
        a = r[EXPERTS_PER_GROUP * g:EXPERTS_PER_GROUP * (g + 1)]
        best = None
        for i in range(EXPERTS_PER_GROUP):
            for j in range(i + 1, EXPERTS_PER_GROUP):
                p = a[i] + a[j]
                best = p if best is None else jnp.maximum(best, p)
        gscore.append(best)
    gbest, gsel = gscore[0], jnp.zeros_like(gscore[0], dtype=jnp.int32)
    for g in range(1, N_GROUPS):
        better = gscore[g] > gbest
        gsel = jnp.where(better, g, gsel)
        gbest = jnp.where(better, gscore[g], gbest)
    neg = jnp.full_like(r[0], -jnp.inf)
    cand = [jnp.where(gsel == e // EXPERTS_PER_GROUP, r[e], neg) for e in range(N_EXPERTS)]

    def first_argmax(vals):
        best, idx, w = vals[0], jnp.zeros_like(gsel), sr[0]
        for e in range(1, N_EXPERTS):
            better = vals[e] > best
            idx = jnp.where(better, e, idx)
            w = jnp.where(better, sr[e], w)
            best = jnp.where(better, vals[e], best)
        return idx, w

    i1, w1 = first_argmax(cand)
    i2, w2 = first_argmax([jnp.where(i1 == e, neg, cand[e]) for e in range(N_EXPERTS)])
    tot = w1 + w2
    g1, g2 = w1 / tot, w2 / tot
    zero = jnp.zeros_like(g1)
    rows = [jnp.where(i1 == e, g1, zero) + jnp.where(i2 == e, g2, zero) for e in range(N_EXPERTS)]
    pad = jnp.zeros((LANES - N_EXPERTS, logits_t.shape[1]), F32)
    return jnp.concatenate(rows + [pad], axis=0)


def _moe_kernel(x_ref, o_ref, mod_ref, g_ref, wo_ref, wr_hi_ref, wr_lo_ref, rb_ref, wgu_ref, wd_ref, fin_ref,
                y_ref, xnew_ref, h_ref, gates_ref, acc_ref, *, final):
    e = pl.program_id(1)

    @pl.when(e == 0)
    def _():
        mod = mod_ref[0]
        xn = x_ref[...] + mod[2:3] * _dot(o_ref[...], wo_ref[...])
        xnew_ref[...] = xn
        h = _rms(xn, g_ref[...]) * (1.0 + mod[4:5]) + mod[3:4]
        h_hi = h.astype(BF16)
        h_lo = (h - h_hi.astype(F32)).astype(BF16)
        h_ref[...] = h_hi
        logits_t = _dot_nt(wr_hi_ref[...], h_hi) + _dot_nt(wr_hi_ref[...], h_lo) + _dot_nt(wr_lo_ref[...], h_hi)
        gates_ref[...] = _route(logits_t, rb_ref[...]).T
        acc_ref[...] = jnp.zeros_like(acc_ref)

    h = h_ref[...]
    au = _dot(h, wgu_ref[...])
    a, u = au[:, :D_EXPERT], au[:, D_EXPERT:]
    lane = lax.broadcasted_iota(jnp.int32, gates_ref.shape, 1)
    gate = jnp.sum(jnp.where(lane == e, gates_ref[...], 0.0), axis=1, keepdims=True)
    act = (a * _sigmoid(a)) * u * gate
    acc_ref[...] += _dot(act.astype(BF16), wd_ref[...])

    @pl.when(e == N_EXPERTS - 1)
    def _():
        out = xnew_ref[...] + mod_ref[0][5:6] * acc_ref[...]
        if final:
            out = _rms(out, fin_ref[...])
        y_ref[...] = out


def _moe(x, o, mods, gain, wo, wr_hi, wr_lo, rbias, wgu, wd, fin, *, latent, tm, final):
    n = x.shape[0]
    const = lambda i, e: (0, 0)
    rows_per_cond = (2048 // tm) if latent else 0
    if rows_per_cond == 0:
        mod_spec = pl.BlockSpec((1, 6, D_MODEL), lambda i, e: (0, 0, 0))
    else:
        mod_spec = pl.BlockSpec((1, 6, D_MODEL), lambda i, e: (1 + i // rows_per_cond, 0, 0))
    return pl.pallas_call(
        functools.partial(_moe_kernel, final=final),
        grid=(n // tm, N_EXPERTS),
        in_specs=[
            pl.BlockSpec((tm, D_MODEL), lambda i, e: (i, 0)),
            pl.BlockSpec((tm, D_MODEL), lambda i, e: (i, 0)),
            mod_spec,
            pl.BlockSpec((1, D_MODEL), const),
            pl.BlockSpec((D_MODEL, D_MODEL), const),
            pl.BlockSpec((LANES, D_MODEL), const),
            pl.BlockSpec((LANES, D_MODEL), const),
            pl.BlockSpec((LANES, 1), const),
            pl.BlockSpec((None, D_MODEL, 2 * D_EXPERT), lambda i, e: (e, 0, 0)),
            pl.BlockSpec((None, D_EXPERT, D_MODEL), lambda i, e: (e, 0, 0)),
            pl.BlockSpec((1, D_MODEL), const),
        ],
        out_specs=pl.BlockSpec((tm, D_MODEL), lambda i, e: (i, 0)),
        out_shape=jax.ShapeDtypeStruct((n, D_MODEL), F32),
        scratch_shapes=[
            pltpu.VMEM((tm, D_MODEL), F32),
            pltpu.VMEM((tm, D_MODEL), BF16),
            pltpu.VMEM((tm, LANES), F32),
            pltpu.VMEM((tm, D_MODEL), F32),
        ],
        compiler_params=_cparams(("parallel", "arbitrary")),
        name="moe_lat" if latent else "moe_ctx",
    )(x, o, mods, gain.reshape(1, D_MODEL), wo, wr_hi, wr_lo, rbias, wgu, wd, fin.reshape(1, D_MODEL))


def _cache_head_layout(c):
    z = jnp.zeros_like(c)
    even = jnp.concatenate([c, z], axis=-1)
    odd = jnp.concatenate([z, c], axis=-1)
    return jnp.stack([even, odd], axis=3).reshape(c.shape[0], c.shape[1], 2 * N_KV_HEADS * LANES).astype(BF16)


def _mla_weights(wdq, wdkv, wuq, wukv):
    wd = jnp.concatenate([wdq, wdkv, jnp.zeros((D_MODEL, MLA_DOWN - MLA_Q_LORA - MLA_KV_LORA - MLA_ROPE), F32)], axis=1)
    dk = MLA_NOPE + MLA_ROPE
    wuq_p = jnp.pad(wuq.reshape(MLA_Q_LORA, MLA_HEADS, dk), ((0, 0), (0, 0), (0, LANES - dk)))
    wuq_p = wuq_p.reshape(MLA_Q_LORA, MLA_HEADS * LANES)
    kv = wukv.reshape(MLA_KV_LORA, MLA_HEADS, MLA_NOPE + MLA_V)
    k_nope, v = kv[..., :MLA_NOPE], kv[..., MLA_NOPE:]
    wk_top = jnp.pad(k_nope, ((0, 0), (0, 0), (0, LANES - MLA_NOPE))).reshape(MLA_KV_LORA, MLA_HEADS * LANES)
    place = np.zeros((MLA_CK - MLA_KV_LORA, MLA_HEADS, LANES), np.float32)
    for r in range(MLA_ROPE):
        place[r, :, MLA_NOPE + r] = 1.0
    wk = jnp.concatenate([wk_top, jnp.asarray(place.reshape(MLA_CK - MLA_KV_LORA, MLA_HEADS * LANES))], axis=0)
    z = jnp.zeros_like(v)
    v_even = jnp.concatenate([v, z], axis=-1)
    v_odd = jnp.concatenate([z, v], axis=-1)
    parity = (jnp.arange(MLA_HEADS) % 2 == 0)[None, :, None]
    wv_top = jnp.where(parity, v_even, v_odd).reshape(MLA_KV_LORA, MLA_HEADS * LANES)
    wv = jnp.concatenate([wv_top, jnp.zeros((MLA_CK - MLA_KV_LORA, MLA_HEADS * LANES), F32)], axis=0)
    return wd.astype(BF16), wuq_p.astype(BF16), wk.astype(BF16), wv.astype(BF16)


def kernel(x_prompt, x_sample, cache_swa_k, cache_swa_v, cache_qkn_k, cache_qkn_v, cache_mla_ckv, cache_mla_kpe, c, c_ctx, w_mod, b_mod, norm_mix, norm_ffn, swa_wqkv, swa_wo, swa_sink, qkn_wqkv, qkn_wo, qkn_qnorm, qkn_knorm, mla_wdq, mla_qnorm, mla_wuq, mla_wdkv, mla_kvnorm, mla_wukv, mla_wo, w_router, router_bias, moe_w_gate, moe_w_up, moe_w_down, final_norm):
    nb, sc, _ = x_prompt.shape
    nd, sl, _ = x_sample.shape
    past = cache_swa_k.shape[2]
    cond = jnp.zeros((COND_ROWS, D_MODEL), F32).at[0].set(c_ctx).at[1:1 + nd].set(c)
    mods = _adaln(cond, w_mod, b_mod)

    gqa_tables = _rope_tables(sl, HEAD_DIM, 0, HEAD_DIM)
    mla_q_tables = _rope_tables(sl, MLA_ROPE, MLA_NOPE, 0)
    mla_k_tables = _rope_tables(sl, MLA_ROPE, 0, 0)

    wr_t = jnp.zeros((LANES, D_MODEL), F32).at[:N_EXPERTS].set(w_router.T)
    wr_hi = wr_t.astype(BF16)
    wr_lo = (wr_t - wr_hi.astype(F32)).astype(BF16)
    rbias = jnp.zeros((LANES, 1), F32).at[:N_EXPERTS, 0].set(router_bias)
    wgu = jnp.concatenate([moe_w_gate, moe_w_up], axis=-1).astype(BF16)
    wdn = moe_w_down.astype(BF16)

    xc = x_prompt.reshape(nb * sc, D_MODEL)
    xl = x_sample.reshape(nd * sl, D_MODEL)
    tm_c, tm_l, tq = 512, 512, 256
    states = {k: [] for k in ("swa_k", "swa_v", "qkn_k", "qkn_v", "ckv", "kpe")}
    for i in range(DEPTH):
        kind, j = i % N_MIXERS, i // N_MIXERS
        m = mods[i]
        if kind in (0, 1):
            if kind == 0:
                w, wo, qn, kn = swa_wqkv[j], swa_wo[j], None, None
                ck, cv, sink = cache_swa_k[:, j], cache_swa_v[:, j], swa_sink[j]
            else:
                w, wo, qn, kn = qkn_wqkv[j], qkn_wo[j], qkn_qnorm[j], qkn_knorm[j]
                ck, cv, sink = cache_qkn_k[:, j], cache_qkn_v[:, j], None
            w = w.astype(BF16)
            qc, kc, vc, k32, v32 = _proj_gqa(xc, m, norm_mix[i], w, qn, kn, None, latent=False, tm=tm_c)
            ql, kl, vl = _proj_gqa(xl, m, norm_mix[i], w, qn, kn, gqa_tables, latent=True, tm=tm_l)
            wide = N_HEADS * HEAD_DIM
            oc = _attention(qc.reshape(nb, sc, wide), [(kc.reshape(nb, sc, wide), vc.reshape(nb, sc, wide))],
                            gqa=True, steps=1, tq=sc, sink=sink)
            ol = _attention(ql.reshape(nd, sl, wide),
                            [(kl.reshape(nd, sl, wide), vl.reshape(nd, sl, wide)),
                             (_cache_head_layout(ck), _cache_head_layout(cv))],
                            gqa=True, steps=N_KV_HEADS, tq=tq, band=(kind == 0), sink=sink)
            names = ("swa_k", "swa_v") if kind == 0 else ("qkn_k", "qkn_v")
            states[names[0]].append(k32.reshape(nb, sc, N_KV_HEADS, HEAD_DIM))
            states[names[1]].append(v32.reshape(nb, sc, N_KV_HEADS, HEAD_DIM))
        else:
            wd, wuq, wk, wv = _mla_weights(mla_wdq[j], mla_wdkv[j], mla_wuq[j], mla_wukv[j])
            wo = mla_wo[j]
            qc, ckc, ckv32, kpe32 = _proj_mla(xc, m, norm_mix[i], wd, mla_qnorm[j], wuq, mla_kvnorm[j],
                                              None, None, latent=False, tm=tm_c)
            ql, ckl = _proj_mla(xl, m, norm_mix[i], wd, mla_qnorm[j], wuq, mla_kvnorm[j],
                                mla_q_tables, mla_k_tables, latent=True, tm=tm_l)
            cache = jnp.concatenate(
                [cache_mla_ckv[:, j], cache_mla_kpe[:, j],
                 jnp.zeros((nd, past, MLA_CK - MLA_KV_LORA - MLA_ROPE), F32)], axis=-1).astype(BF16)
            ck_all = jnp.concatenate([ckl.reshape(nd, sl, MLA_CK), cache], axis=1)
            wq = MLA_HEADS * LANES
            kc, vc = _kv_expand(ckc, wk, wv, tm_c)
            kl, vl = _kv_expand(ck_all.reshape(nd * (sl + past), MLA_CK), wk, wv, 512)
            oc = _attention(qc.reshape(nb, sc, wq), [(kc.reshape(nb, sc, wq), vc.reshape(nb, sc, wq))],
                            gqa=False, steps=1, tq=sc)
            ol = _attention(ql.reshape(nd, sl, wq), [(kl.reshape(nd, sl + past, wq), vl.reshape(nd, sl + past, wq))],
                            gqa=False, steps=MLA_HEADS // 2, tq=tq)
            states["ckv"].append(ckv32.reshape(nb, sc, MLA_KV_LORA))
            states["kpe"].append(kpe32.reshape(nb, sc, MLA_ROPE))
        final = i == DEPTH - 1
        args = (m, norm_ffn[i], wo.astype(BF16), wr_hi, wr_lo, rbias, wgu[i], wdn[i], final_norm)
        xc = _moe(xc, oc.reshape(nb * sc, D_MODEL), *args, latent=False, tm=tm_c, final=final)
        xl = _moe(xl, ol.reshape(nd * sl, D_MODEL), *args, latent=True, tm=tm_l, final=final)
    return (xc.reshape(nb, sc, D_MODEL), xl.reshape(nd, sl, D_MODEL),
            jnp.stack(states["swa_k"], axis=1), jnp.stack(states["swa_v"], axis=1),
            jnp.stack(states["qkn_k"], axis=1), jnp.stack(states["qkn_v"], axis=1),
            jnp.stack(states["ckv"], axis=1), jnp.stack(states["kpe"], axis=1))
```

```python
import functools

import numpy as np
import jax
import jax.numpy as jnp
from jax import lax
from jax.experimental import pallas as pl
from jax.experimental.pallas import tpu as pltpu

F32 = jnp.float32
BF16 = jnp.bfloat16

D_MODEL = 1024
DEPTH = 4
GRID_W = 64
N_MIXERS = 3
N_HEADS = 16
N_KV_HEADS = 4
HEAD_DIM = 64
QKV_DIM = (N_HEADS + 2 * N_KV_HEADS) * HEAD_DIM
WINDOW = 128
ROPE_THETA = 10000.0
MLA_HEADS = 16
MLA_Q_LORA = 384
MLA_KV_LORA = 256
MLA_NOPE = 64
MLA_ROPE = 32
MLA_V = 64
N_EXPERTS = 16
N_GROUPS = 4
EXPERTS_PER_GROUP = N_EXPERTS // N_GROUPS
D_EXPERT = 256
EPS = 1e-6

LANES = 128
COND_ROWS = 16
MLA_DOWN = 768
MLA_CK = 384
MOE_BLOCK = 1024
MOE_ROWS = 128
MOE_UNROLL = 8
NEG_BIG = -1e30
VMEM_LIMIT = 56 * 1024 * 1024


def _cparams(sem):
    return pltpu.CompilerParams(dimension_semantics=sem, vmem_limit_bytes=VMEM_LIMIT)


def _sigmoid(x):
    return 1.0 / (1.0 + jnp.exp(-x))


def _rms(x, g):
    return x * lax.rsqrt(jnp.mean(x * x, axis=-1, keepdims=True) + EPS) * g


def _dot(a, b):
    return jnp.dot(a, b, preferred_element_type=F32)


def _dot_nt(a, b):
    return lax.dot_general(a, b, (((1,), (1,)), ((), ())), preferred_element_type=F32)


def _adaln_kernel(cond_ref, w_ref, b_ref, o_ref):
    c = cond_ref[...]
    s = (c * _sigmoid(c)).astype(BF16)
    o_ref[...] = _dot(s, w_ref[...].astype(BF16)) + b_ref[...]


def _adaln(cond, w_mod, b_mod):
    tn = 1536
    n = 6 * D_MODEL
    out = pl.pallas_call(
        _adaln_kernel,
        grid=(DEPTH, n // tn),
        in_specs=[
            pl.BlockSpec((COND_ROWS, D_MODEL), lambda l, j: (0, 0)),
            pl.BlockSpec((None, D_MODEL, tn), lambda l, j: (l, 0, j)),
            pl.BlockSpec((None, 1, tn), lambda l, j: (l, 0, j)),
        ],
        out_specs=pl.BlockSpec((None, COND_ROWS, tn), lambda l, j: (l, 0, j)),
        out_shape=jax.ShapeDtypeStruct((DEPTH, COND_ROWS, n), F32),
        compiler_params=_cparams(("parallel", "parallel")),
        name="adaln",
    )(cond, w_mod, b_mod.reshape(DEPTH, 1, n))
    return out.reshape(DEPTH, COND_ROWS, 6, D_MODEL)


def _mod_spec(rows_per_cond):
    if rows_per_cond == 0:
        return pl.BlockSpec((1, 6, D_MODEL), lambda i: (0, 0, 0))
    return pl.BlockSpec((1, 6, D_MODEL), lambda i: (1 + i // rows_per_cond, 0, 0))


def _rope_tables(seq, head_dim, lane0, period):
    half, quarter = head_dim // 2, head_dim // 4
    pos = np.arange(seq)
    row, col = (pos // GRID_W).astype(np.float64), (pos % GRID_W).astype(np.float64)
    inv = ROPE_THETA ** (-np.arange(quarter, dtype=np.float64) / quarter)
    ang = np.concatenate([row[:, None] * inv, col[:, None] * inv], axis=-1)
    cos = np.ones((seq, LANES))
    s1 = np.zeros((seq, LANES))
    s2 = np.zeros((seq, LANES))
    starts = [lane0] if period == 0 else list(range(lane0, LANES, period))
    for st in starts:
        cos[:, st:st + half] = np.cos(ang)
        cos[:, st + half:st + head_dim] = np.cos(ang)
        s1[:, st:st + half] = -np.sin(ang)
        s2[:, st + half:st + head_dim] = np.sin(ang)
    return tuple(jnp.asarray(t, F32) for t in (cos, s1, s2))


def _rope_chunk(x, cos, s1, s2, half):
    return x * cos + pltpu.roll(x, LANES - half, 1) * s1 + pltpu.roll(x, half, 1) * s2


def _head_layout(kv):
    lo = lax.broadcasted_iota(jnp.int32, (kv.shape[0], LANES), 1) < HEAD_DIM
    zero = jnp.zeros((kv.shape[0], LANES), F32)
    out = []
    for j in range(2):
        ch = kv[:, LANES * j:LANES * (j + 1)]
        ro = pltpu.roll(ch, HEAD_DIM, 1)
        out += [jnp.where(lo, ch, zero), jnp.where(lo, zero, ro), jnp.where(lo, ro, zero), jnp.where(lo, zero, ch)]
    return jnp.concatenate(out, axis=1)


def _proj_gqa_kernel(*refs, qknorm, rope, state):
    it = iter(refs)
    x_ref, mod_ref, g_ref, w_ref = next(it), next(it), next(it), next(it)
    if qknorm:
        qn_ref, kn_ref, gm_ref = next(it), next(it), next(it)
    if rope:
        cos_ref, s1_ref, s2_ref = next(it), next(it), next(it)
    q_ref, kp_ref, vp_ref = next(it), next(it), next(it)
    if state:
        k32_ref, v32_ref = next(it), next(it)

    mod = mod_ref[0]
    h = _rms(x_ref[...], g_ref[...]) * (1.0 + mod[1:2]) + mod[0:1]
    qkv = _dot(h.astype(BF16), w_ref[...])
    nq = N_HEADS * HEAD_DIM
    nk = N_KV_HEADS * HEAD_DIM
    chunks = [qkv[:, LANES * j:LANES * (j + 1)] for j in range((nq + nk) // LANES)]
    if qknorm:
        gains = [qn_ref[...]] * (nq // LANES) + [kn_ref[...]] * (nk // LANES)
        gm = gm_ref[...]
        chunks = [c * lax.rsqrt(_dot((c * c).astype(BF16), gm) + EPS) * g for c, g in zip(chunks, gains)]
    k_state = jnp.concatenate(chunks[nq // LANES:], axis=1)
    if rope:
        cos, s1, s2 = cos_ref[...], s1_ref[...], s2_ref[...]
        chunks = [_rope_chunk(c, cos, s1, s2, HEAD_DIM // 2) for c in chunks]
    q = jnp.concatenate(chunks[:nq // LANES], axis=1) * (HEAD_DIM ** -0.5)
    k = jnp.concatenate(chunks[nq // LANES:], axis=1)
    v = qkv[:, nq + nk:]
    q_ref[...] = q.astype(BF16)
    kp_ref[...] = _head_layout(k).astype(BF16)
    vp_ref[...] = _head_layout(v).astype(BF16)
    if state:
        k32_ref[...] = k_state
        v32_ref[...] = v


def _proj_gqa(x, mods, gain, w, qn, kn, tables, *, latent, tm):
    n = x.shape[0]
    qknorm = qn is not None
    tiles_per_seq = 2048 // tm
    ins = [x, mods, gain.reshape(1, D_MODEL), w]
    specs = [
        pl.BlockSpec((tm, D_MODEL), lambda i: (i, 0)),
        _mod_spec(tiles_per_seq if latent else 0),
        pl.BlockSpec((1, D_MODEL), lambda i: (0, 0)),
        pl.BlockSpec((D_MODEL, QKV_DIM), lambda i: (0, 0)),
    ]
    if qknorm:
        gm = np.kron(np.eye(2), np.full((HEAD_DIM, HEAD_DIM), 1.0 / HEAD_DIM))
        ins += [jnp.tile(qn, 2).reshape(1, LANES), jnp.tile(kn, 2).reshape(1, LANES), jnp.asarray(gm, BF16)]
        specs += [pl.BlockSpec((1, LANES), lambda i: (0, 0))] * 2 + [pl.BlockSpec((LANES, LANES), lambda i: (0, 0))]
    if latent:
        ins += list(tables)
        specs += [pl.BlockSpec((tm, LANES), lambda i: (i % tiles_per_seq, 0))] * 3
    wide = N_HEADS * HEAD_DIM
    outs = [jax.ShapeDtypeStruct((n, wide), BF16)] * 3
    ospecs = [pl.BlockSpec((tm, wide), lambda i: (i, 0))] * 3
    if not latent:
        nk = N_KV_HEADS * HEAD_DIM
        outs += [jax.ShapeDtypeStruct((n, nk), F32)] * 2
        ospecs += [pl.BlockSpec((tm, nk), lambda i: (i, 0))] * 2
    return pl.pallas_call(
        functools.partial(_proj_gqa_kernel, qknorm=qknorm, rope=latent, state=not latent),
        grid=(n // tm,),
        in_specs=specs,
        out_specs=ospecs,
        out_shape=outs,
        compiler_params=_cparams(("parallel",)),
        name="proj_gqa_lat" if latent else "proj_gqa_ctx",
    )(*ins)


def _proj_mla_kernel(*refs, rope, state):
    it = iter(refs)
    x_ref, mod_ref, g_ref, wd_ref, qn_ref, wuq_ref, kvn_ref = (next(it) for _ in range(7))
    if rope:
        qt = [next(it) for _ in range(3)]
        kt = [next(it) for _ in range(3)]
    q_ref, ck_ref = next(it), next(it)
    if state:
        ckv_ref, kpe_ref = next(it), next(it)

    mod = mod_ref[0]
    h = _rms(x_ref[...], g_ref[...]) * (1.0 + mod[1:2]) + mod[0:1]
    d = _dot(h.astype(BF16), wd_ref[...])
    cq = _rms(d[:, :MLA_Q_LORA], qn_ref[...])
    q = _dot(cq.astype(BF16), wuq_ref[...])
    ckv = _rms(d[:, MLA_Q_LORA:MLA_Q_LORA + MLA_KV_LORA], kvn_ref[...])
    kpe = d[:, MLA_Q_LORA + MLA_KV_LORA:]
    kpe_state = kpe
    scale = (MLA_NOPE + MLA_ROPE) ** -0.5
    qs = [q[:, LANES * j:LANES * (j + 1)] for j in range(MLA_HEADS)]
    if rope:
        cos, s1, s2 = (t[...] for t in qt)
        qs = [_rope_chunk(c, cos, s1, s2, MLA_ROPE // 2) for c in qs]
        kpe = _rope_chunk(kpe, *(t[...] for t in kt), MLA_ROPE // 2)
    q_ref[...] = (jnp.concatenate(qs, axis=1) * scale).astype(BF16)
    ck_ref[...] = jnp.concatenate([ckv, kpe], axis=1).astype(BF16)
    if state:
        ckv_ref[...] = ckv
        kpe_ref[...] = kpe_state[:, :MLA_ROPE]


def _proj_mla(x, mods, gain, wd, qn, wuq, kvn, qtables, ktables, *, latent, tm):
    n = x.shape[0]
    tiles_per_seq = 2048 // tm
    wq = MLA_HEADS * LANES
    ins = [x, mods, gain.reshape(1, D_MODEL), wd, qn.reshape(1, MLA_Q_LORA), wuq, kvn.reshape(1, MLA_KV_LORA)]
    specs = [
        pl.BlockSpec((tm, D_MODEL), lambda i: (i, 0)),
        _mod_spec(tiles_per_seq if latent else 0),
        pl.BlockSpec((1, D_MODEL), lambda i: (0, 0)),
        pl.BlockSpec((D_MODEL, MLA_DOWN), lambda i: (0, 0)),
        pl.BlockSpec((1, MLA_Q_LORA), lambda i: (0, 0)),
        pl.BlockSpec((MLA_Q_LORA, wq), lambda i: (0, 0)),
        pl.BlockSpec((1, MLA_KV_LORA), lambda i: (0, 0)),
    ]
    if latent:
        ins += list(qtables) + list(ktables)
        specs += [pl.BlockSpec((tm, LANES), lambda i: (i % tiles_per_seq, 0))] * 6
    outs = [jax.ShapeDtypeStruct((n, wq), BF16), jax.ShapeDtypeStruct((n, MLA_CK), BF16)]
    ospecs = [pl.BlockSpec((tm, wq), lambda i: (i, 0)), pl.BlockSpec((tm, MLA_CK), lambda i: (i, 0))]
    if not latent:
        outs += [jax.ShapeDtypeStruct((n, MLA_KV_LORA), F32), jax.ShapeDtypeStruct((n, MLA_ROPE), F32)]
        ospecs += [pl.BlockSpec((tm, MLA_KV_LORA), lambda i: (i, 0)), pl.BlockSpec((tm, MLA_ROPE), lambda i: (i, 0))]
    return pl.pallas_call(
        functools.partial(_proj_mla_kernel, rope=latent, state=not latent),
        grid=(n // tm,),
        in_specs=specs,
        out_specs=ospecs,
        out_shape=outs,
        compiler_params=_cparams(("parallel",)),
        name="proj_mla_lat" if latent else "proj_mla_ctx",
    )(*ins)


def _kv_expand_kernel(ck_ref, wk_ref, wv_ref, k_ref, v_ref):
    ck = ck_ref[...]
    k_ref[...] = _dot(ck, wk_ref[...]).astype(BF16)
    v_ref[...] = _dot(ck, wv_ref[...]).astype(BF16)


def _kv_expand(ck, wk, wv, tm):
    n = ck.shape[0]
    wide = MLA_HEADS * LANES
    return pl.pallas_call(
        _kv_expand_kernel,
        grid=(n // tm,),
        in_specs=[
            pl.BlockSpec((tm, MLA_CK), lambda i: (i, 0)),
            pl.BlockSpec((MLA_CK, wide), lambda i: (0, 0)),
            pl.BlockSpec((MLA_CK, wide), lambda i: (0, 0)),
        ],
        out_specs=[pl.BlockSpec((tm, wide), lambda i: (i, 0))] * 2,
        out_shape=[jax.ShapeDtypeStruct((n, wide), BF16)] * 2,
        compiler_params=_cparams(("parallel",)),
        name="mla_kv_expand",
    )(ck, wk, wv)


def _attn_kernel(*refs, pairs, n_seg, tq, band, use_sink, heads_per_step):
    it = iter(refs)
    if use_sink:
        sink_ref = next(it)
    q_ref = next(it)
    kv_refs = [(next(it), next(it)) for _ in range(n_seg)]
    o_ref = next(it)

    if band:
        qi = pl.program_id(2)
        seq = kv_refs[0][0].shape[0]
        win = tq + 2 * WINDOW
        start = pl.multiple_of(jnp.clip(qi * tq - WINDOW, 0, seq - win), WINDOW)
        q_pos = qi * tq + lax.broadcasted_iota(jnp.int32, (tq, win), 0)
        k_pos = start + lax.broadcasted_iota(jnp.int32, (tq, win), 1)
        near = jnp.abs(k_pos - q_pos) <= WINDOW

    for pi, (qe, qo, ke, ko, oo) in enumerate(pairs):
        acc = None
        for parity, (qoff, koff) in enumerate(((qe, ke), (qo, ko))):
            qh = q_ref[:, qoff:qoff + LANES]
            scores, values = [], []
            for si, (k_ref, v_ref) in enumerate(kv_refs):
                if band and si == 0:
                    kh = k_ref[pl.ds(start, win), koff:koff + LANES]
                    vh = v_ref[pl.ds(start, win), koff:koff + LANES]
                    s = jnp.where(near, _dot_nt(qh, kh), NEG_BIG)
                else:
                    kh = k_ref[:, koff:koff + LANES]
                    vh = v_ref[:, koff:koff + LANES]
                    s = _dot_nt(qh, kh)
                scores.append(s)
                values.append(vh)
            m = functools.reduce(jnp.maximum, [jnp.max(s, axis=-1, keepdims=True) for s in scores])
            if use_sink:
                sk = sink_ref[pl.program_id(1) * heads_per_step + 2 * pi + parity]
                m = jnp.maximum(m, sk)
            es = [jnp.exp(s - m) for s in scores]
            denom = functools.reduce(jnp.add, [jnp.sum(e, axis=-1, keepdims=True) for e in es])
            if use_sink:
                denom = denom + jnp.exp(sk - m)
            pv = functools.reduce(jnp.add, [_dot(e.astype(BF16), vh) for e, vh in zip(es, values)])
            pv = pv / denom
            acc = pv if acc is None else acc + pv
        o_ref[:, oo:oo + LANES] = acc.astype(o_ref.dtype)


def _attention(q, segs, *, gqa, steps, tq, band=False, sink=None):
    b, sq, wq_total = q.shape
    wk_total = segs[0][0].shape[2]
    wo_total = N_HEADS * HEAD_DIM
    wq, wk, wo = wq_total // steps, wk_total // steps, wo_total // steps
    n_pairs = wo // LANES
    if gqa:
        pairs = [(LANES * p, LANES * p, 2 * LANES * (p // 2), 2 * LANES * (p // 2) + LANES, LANES * p)
                 for p in range(n_pairs)]
    else:
        pairs = [(2 * LANES * p, 2 * LANES * p + LANES, 2 * LANES * p, 2 * LANES * p + LANES, LANES * p)
                 for p in range(n_pairs)]
    ins, specs = [], []
    if sink is not None:
        ins.append(sink)
        specs.append(pl.BlockSpec(memory_space=pltpu.SMEM))
    ins.append(q)
    specs.append(pl.BlockSpec((None, tq, wq), lambda bi, si, qi: (bi, qi, si)))
    for k, v in segs:
        sk = k.shape[1]
        ins += [k, v]
        specs += [pl.BlockSpec((None, sk, wk), lambda bi, si, qi: (bi, 0, si))] * 2
    return pl.pallas_call(
        functools.partial(_attn_kernel, pairs=pairs, n_seg=len(segs), tq=tq, band=band,
                          use_sink=sink is not None, heads_per_step=2 * n_pairs),
        grid=(b, steps, sq // tq),
        in_specs=specs,
        out_specs=pl.BlockSpec((None, tq, wo), lambda bi, si, qi: (bi, qi, si)),
        out_shape=jax.ShapeDtypeStruct((b, sq, wo_total), BF16),
        compiler_params=_cparams(("parallel", "parallel", "parallel")),
        name="attention",
    )(*ins)


def _route(logits_t, bias_col):
    s = _sigmoid(logits_t)
    sb = s + bias_col
    r = [sb[e:e + 1, :] for e in range(N_EXPERTS)]
    sr = [s[e:e + 1, :] for e in range(N_EXPERTS)]
    gscore = []
    for g in range(N_GROUPS):
        a = r[EXPERTS_PER_GROUP * g:EXPERTS_PER_GROUP * (g + 1)]
        best = None
        for i in range(EXPERTS_PER_GROUP):
            for j in range(i + 1, EXPERTS_PER_GROUP):
                p = a[i] + a[j]
                best = p if best is None else jnp.maximum(best, p)
        gscore.append(best)
    gbest, gsel = gscore[0], jnp.zeros_like(gscore[0], dtype=jnp.int32)
    for g in range(1, N_GROUPS):
        better = gscore[g] > gbest
        gsel = jnp.where(better, g, gsel)
        gbest = jnp.where(better, gscore[g], gbest)
    neg = jnp.full_like(r[0], -jnp.inf)
    cand = [jnp.where(gsel == e // EXPERTS_PER_GROUP, r[e], neg) for e in range(N_EXPERTS)]

    def first_argmax(vals):
        best, idx, w = vals[0], jnp.zeros_like(gsel), sr[0]
        for e in range(1, N_EXPERTS):
            better = vals[e] > best
            idx = jnp.where(better, e, idx)
            w = jnp.where(better, sr[e], w)
            best = jnp.where(better, vals[e], best)
        return idx, w

    i1, w1 = first_argmax(cand)
    i2, w2 = first_argmax([jnp.where(i1 == e, neg, cand[e]) for e in range(N_EXPERTS)])
    tot = w1 + w2
    return i1, i2, w1 / tot, w2 / tot


def _moe_pre_kernel(x_ref, o_ref, mod_ref, g_ref, wo_ref, wr_hi_ref, wr_lo_ref, rb_ref, xnew_ref, idx_ref, wts_ref):
    mod = mod_ref[0]
    xn = x_ref[...] + mod[2:3] * _dot(o_ref[...], wo_ref[...])
    xnew_ref[...] = xn
    h = _rms(xn, g_ref[...]) * (1.0 + mod[4:5]) + mod[3:4]
    h_hi = h.astype(BF16)
    h_lo = (h - h_hi.astype(F32)).astype(BF16)
    logits_t = _dot_nt(wr_hi_ref[...], h_hi) + _dot_nt(wr_hi_ref[...], h_lo) + _dot_nt(wr_lo_ref[...], h_hi)
    i1, i2, g1, g2 = _route(logits_t, rb_ref[...])
    idx_ref[...] = jnp.concatenate([i1, i2], axis=0)
    wts_ref[...] = jnp.concatenate([g1, g2], axis=0)


def _moe_pre(x, o, mods, gain, wo, wr_hi, wr_lo, rbias, *, latent, tm):
    n = x.shape[0]
    const = lambda i: (0, 0)
    return pl.pallas_call(
        _moe_pre_kernel,
        grid=(n // tm,),
        in_specs=[
            pl.BlockSpec((tm, D_MODEL), lambda i: (i, 0)),
            pl.BlockSpec((tm, D_MODEL), lambda i: (i, 0)),
            _mod_spec((2048 // tm) if latent else 0),
            pl.BlockSpec((1, D_MODEL), const),
            pl.BlockSpec((D_MODEL, D_MODEL), const),
            pl.BlockSpec((LANES, D_MODEL), const),
            pl.BlockSpec((LANES, D_MODEL), const),
            pl.BlockSpec((LANES, 1), const),
        ],
        out_specs=[
            pl.BlockSpec((tm, D_MODEL), lambda i: (i, 0)),
            pl.BlockSpec((None, 2, tm), lambda i: (i, 0, 0)),
            pl.BlockSpec((None, 2, tm), lambda i: (i, 0, 0)),
        ],
        out_shape=[
            jax.ShapeDtypeStruct((n, D_MODEL), F32),
            jax.ShapeDtypeStruct((n // tm, 2, tm), jnp.int32),
            jax.ShapeDtypeStruct((n // tm, 2, tm), F32),
        ],
        compiler_params=_cparams(("parallel",)),
        name="moe_pre_lat" if latent else "moe_pre_ctx",
    )(x, o, mods, gain.reshape(1, D_MODEL), wo, wr_hi, wr_lo, rbias)


def _moe_plan(idx, tb):
    n = idx.shape[0] * idx.shape[2]
    e1 = idx[:, 0, :].reshape(n // tb, tb)
    e2 = idx[:, 1, :].reshape(n // tb, tb)
    hot1 = jax.nn.one_hot(e1, N_EXPERTS, dtype=jnp.int32)
    hot2 = jax.nn.one_hot(e2, N_EXPERTS, dtype=jnp.int32)
    member = hot1 + hot2
    csum = jnp.cumsum(member, axis=1)
    rank = csum - member
    chunks = (csum[:, -1, :] + MOE_ROWS - 1) // MOE_ROWS
    ends = jnp.cumsum(chunks * MOE_ROWS, axis=-1)
    starts = ends - chunks * MOE_ROWS
    pos = starts[:, None, :] + rank
    slot1 = jnp.sum(pos * hot1, axis=-1)
    slot2 = jnp.sum(pos * hot2, axis=-1)
    return slot1.reshape(n), slot2.reshape(n), starts.reshape(-1), chunks.reshape(-1)


def _moe_kernel(slot1_ref, slot2_ref, w1_ref, w2_ref, start_ref, chunks_ref,
                xnew_ref, mod_ref, g_ref, wgu_ref, wd_ref, fin_ref, out_ref,
                order_ref, tok_ref, lhs_ref, y_ref, *, tb, final):
    b, e = pl.program_id(0), pl.program_id(1)
    tok0 = b * tb
    sub = D_MODEL // LANES

    @pl.when(e == 0)
    def _():
        @pl.when(b == 0)
        def _():
            def zero(i, carry):
                order_ref[i] = 0
                return carry
            lax.fori_loop(0, order_ref.shape[0], zero, 0)

        mod = mod_ref[0]
        h = _rms(xnew_ref[...], g_ref[...]) * (1.0 + mod[4:5]) + mod[3:4]
        for s in range(sub):
            tok_ref[pl.ds(s, tb, stride=sub), :] = h[:, LANES * s:LANES * (s + 1)]

        def scatter(i, carry):
            for k in range(MOE_UNROLL):
                t = i * MOE_UNROLL + k
                order_ref[slot1_ref[tok0 + t]] = t
                order_ref[slot2_ref[tok0 + t]] = t
            return carry
        lax.fori_loop(0, tb // MOE_UNROLL, scatter, 0)

    start = start_ref[b * N_EXPERTS + e]

    def chunk(c, carry):
        base = start + c * MOE_ROWS
        for r in range(MOE_ROWS):
            t = order_ref[base + r]
            lhs_ref[sub * r:sub * (r + 1), :] = tok_ref[pl.ds(pl.multiple_of(t * sub, sub), sub), :]
        lhs = jnp.concatenate([lhs_ref[pl.ds(s, MOE_ROWS, stride=sub), :] for s in range(sub)], axis=1)
        au = _dot(lhs.astype(BF16), wgu_ref[...])
        a, u = au[:, :D_EXPERT], au[:, D_EXPERT:]
        y = _dot((a * _sigmoid(a) * u).astype(BF16), wd_ref[...])
        rows = y_ref.at[pl.ds(pl.multiple_of(base * sub, sub), MOE_ROWS * sub)]
        for s in range(sub):
            rows[pl.ds(s, MOE_ROWS, stride=sub), :] = y[:, LANES * s:LANES * (s + 1)]
        return carry
    lax.fori_loop(0, chunks_ref[b * N_EXPERTS + e], chunk, 0)

    @pl.when(e == N_EXPERTS - 1)
    def _():
        def combine(i, carry):
            for k in range(MOE_UNROLL):
                t = i * MOE_UNROLL + k
                s1 = pl.multiple_of(slot1_ref[tok0 + t] * sub, sub)
                s2 = pl.multiple_of(slot2_ref[tok0 + t] * sub, sub)
                tok_ref[pl.ds(pl.multiple_of(t * sub, sub), sub), :] = (
                    w1_ref[tok0 + t] * y_ref[pl.ds(s1, sub), :] + w2_ref[tok0 + t] * y_ref[pl.ds(s2, sub), :])
            return carry
        lax.fori_loop(0, tb // MOE_UNROLL, combine, 0)
        moe = jnp.concatenate([tok_ref[pl.ds(s, tb, stride=sub), :] for s in range(sub)], axis=1)
        out = xnew_ref[...] + mod_ref[0][5:6] * moe
        if final:
            out = _rms(out, fin_ref[...])
        out_ref[...] = out


def _moe(xnew, idx, wts, mods, gain, wgu, wd, fin, *, latent, final):
    n = xnew.shape[0]
    tb = MOE_BLOCK
    slot1, slot2, starts, chunks = _moe_plan(idx, tb)
    w1 = wts[:, 0, :].reshape(n)
    w2 = wts[:, 1, :].reshape(n)
    cap = 2 * tb + N_EXPERTS * MOE_ROWS
    sub = D_MODEL // LANES
    const = lambda i, e: (0, 0)
    rows_per_cond = (2048 // tb) if latent else 0
    if rows_per_cond == 0:
        mod_spec = pl.BlockSpec((1, 6, D_MODEL), lambda i, e: (0, 0, 0))
    else:
        mod_spec = pl.BlockSpec((1, 6, D_MODEL), lambda i, e: (1 + i // rows_per_cond, 0, 0))
    smem = pl.BlockSpec(memory_space=pltpu.SMEM)
    return pl.pallas_call(
        functools.partial(_moe_kernel, tb=tb, final=final),
        grid=(n // tb, N_EXPERTS),
        in_specs=[smem] * 6 + [
            pl.BlockSpec((tb, D_MODEL), lambda i, e: (i, 0)),
            mod_spec,
            pl.BlockSpec((1, D_MODEL), const),
            pl.BlockSpec((None, D_MODEL, 2 * D_EXPERT), lambda i, e: (e, 0, 0)),
            pl.BlockSpec((None, D_EXPERT, D_MODEL), lambda i, e: (e, 0, 0)),
            pl.BlockSpec((1, D_MODEL), const),
        ],
        out_specs=pl.BlockSpec((tb, D_MODEL), lambda i, e: (i, 0)),
        out_shape=jax.ShapeDtypeStruct((n, D_MODEL), F32),
        scratch_shapes=[
            pltpu.SMEM((cap,), jnp.int32),
            pltpu.VMEM((tb * sub, LANES), F32),
            pltpu.VMEM((MOE_ROWS * sub, LANES), F32),
            pltpu.VMEM((cap * sub, LANES), F32),
        ],
        compiler_params=_cparams(("arbitrary", "arbitrary")),
        name="moe_lat" if latent else "moe_ctx",
    )(slot1, slot2, w1, w2, starts, chunks, xnew, mods, gain.reshape(1, D_MODEL), wgu, wd, fin.reshape(1, D_MODEL))


def _cache_head_layout(c):
    z = jnp.zeros_like(c)
    even = jnp.concatenate([c, z], axis=-1)
    odd = jnp.concatenate([z, c], axis=-1)
    return jnp.stack([even, odd], axis=3).reshape(c.shape[0], c.shape[1], 2 * N_KV_HEADS * LANES).astype(BF16)


def _mla_weights(wdq, wdkv, wuq, wukv):
    wd = jnp.concatenate([wdq, wdkv, jnp.zeros((D_MODEL, MLA_DOWN - MLA_Q_LORA - MLA_KV_LORA - MLA_ROPE), F32)], axis=1)
    dk = MLA_NOPE + MLA_ROPE
    wuq_p = jnp.pad(wuq.reshape(MLA_Q_LORA, MLA_HEADS, dk), ((0, 0), (0, 0), (0, LANES - dk)))
    wuq_p = wuq_p.reshape(MLA_Q_LORA, MLA_HEADS * LANES)
    kv = wukv.reshape(MLA_KV_LORA, MLA_HEADS, MLA_NOPE + MLA_V)
    k_nope, v = kv[..., :MLA_NOPE], kv[..., MLA_NOPE:]
    wk_top = jnp.pad(k_nope, ((0, 0), (0, 0), (0, LANES - MLA_NOPE))).reshape(MLA_KV_LORA, MLA_HEADS * LANES)
    place = np.zeros((MLA_CK - MLA_KV_LORA, MLA_HEADS, LANES), np.float32)
    for r in range(MLA_ROPE):
        place[r, :, MLA_NOPE + r] = 1.0
    wk = jnp.concatenate([wk_top, jnp.asarray(place.reshape(MLA_CK - MLA_KV_LORA, MLA_HEADS * LANES))], axis=0)
    z = jnp.zeros_like(v)
    v_even = jnp.concatenate([v, z], axis=-1)
    v_odd = jnp.concatenate([z, v], axis=-1)
    parity = (jnp.arange(MLA_HEADS) % 2 == 0)[None, :, None]
    wv_top = jnp.where(parity, v_even, v_odd).reshape(MLA_KV_LORA, MLA_HEADS * LANES)
    wv = jnp.concatenate([wv_top, jnp.zeros((MLA_CK - MLA_KV_LORA, MLA_HEADS * LANES), F32)], axis=0)
    return wd.astype(BF16), wuq_p.astype(BF16), wk.astype(BF16), wv.astype(BF16)


def kernel(x_prompt, x_sample, cache_swa_k, cache_swa_v, cache_qkn_k, cache_qkn_v, cache_mla_ckv, cache_mla_kpe, c, c_ctx, w_mod, b_mod, norm_mix, norm_ffn, swa_wqkv, swa_wo, swa_sink, qkn_wqkv, qkn_wo, qkn_qnorm, qkn_knorm, mla_wdq, mla_qnorm, mla_wuq, mla_wdkv, mla_kvnorm, mla_wukv, mla_wo, w_router, router_bias, moe_w_gate, moe_w_up, moe_w_down, final_norm):
    nb, sc, _ = x_prompt.shape
    nd, sl, _ = x_sample.shape
    past = cache_swa_k.shape[2]
    cond = jnp.zeros((COND_ROWS, D_MODEL), F32).at[0].set(c_ctx).at[1:1 + nd].set(c)
    mods = _adaln(cond, w_mod, b_mod)

    gqa_tables = _rope_tables(sl, HEAD_DIM, 0, HEAD_DIM)
    mla_q_tables = _rope_tables(sl, MLA_ROPE, MLA_NOPE, 0)
    mla_k_tables = _rope_tables(sl, MLA_ROPE, 0, 0)

    wr_t = jnp.zeros((LANES, D_MODEL), F32).at[:N_EXPERTS].set(w_router.T)
    wr_hi = wr_t.astype(BF16)
    wr_lo = (wr_t - wr_hi.astype(F32)).astype(BF16)
    rbias = jnp.zeros((LANES, 1), F32).at[:N_EXPERTS, 0].set(router_bias)
    wgu = jnp.concatenate([moe_w_gate, moe_w_up], axis=-1).astype(BF16)
    wdn = moe_w_down.astype(BF16)

    xc = x_prompt.reshape(nb * sc, D_MODEL)
    xl = x_sample.reshape(nd * sl, D_MODEL)
    tm_c, tm_l, tq = 512, 512, 256
    states = {k: [] for k in ("swa_k", "swa_v", "qkn_k", "qkn_v", "ckv", "kpe")}
    for i in range(DEPTH):
        kind, j = i % N_MIXERS, i // N_MIXERS
        m = mods[i]
        if kind in (0, 1):
            if kind == 0:
                w, wo, qn, kn = swa_wqkv[j], swa_wo[j], None, None
                ck, cv, sink = cache_swa_k[:, j], cache_swa_v[:, j], swa_sink[j]
            else:
                w, wo, qn, kn = qkn_wqkv[j], qkn_wo[j], qkn_qnorm[j], qkn_knorm[j]
                ck, cv, sink = cache_qkn_k[:, j], cache_qkn_v[:, j], None
            w = w.astype(BF16)
            qc, kc, vc, k32, v32 = _proj_gqa(xc, m, norm_mix[i], w, qn, kn, None, latent=False, tm=tm_c)
            ql, kl, vl = _proj_gqa(xl, m, norm_mix[i], w, qn, kn, gqa_tables, latent=True, tm=tm_l)
            wide = N_HEADS * HEAD_DIM
            oc = _attention(qc.reshape(nb, sc, wide), [(kc.reshape(nb, sc, wide), vc.reshape(nb, sc, wide))],
                            gqa=True, steps=1, tq=sc, sink=sink)
            ol = _attention(ql.reshape(nd, sl, wide),
                            [(kl.reshape(nd, sl, wide), vl.reshape(nd, sl, wide)),
                             (_cache_head_layout(ck), _cache_head_layout(cv))],
                            gqa=True, steps=N_KV_HEADS, tq=tq, band=(kind == 0), sink=sink)
            names = ("swa_k", "swa_v") if kind == 0 else ("qkn_k", "qkn_v")
            states[names[0]].append(k32.reshape(nb, sc, N_KV_HEADS, HEAD_DIM))
            states[names[1]].append(v32.reshape(nb, sc, N_KV_HEADS, HEAD_DIM))
        else:
            wd, wuq, wk, wv = _mla_weights(mla_wdq[j], mla_wdkv[j], mla_wuq[j], mla_wukv[j])
            wo = mla_wo[j]
            qc, ckc, ckv32, kpe32 = _proj_mla(xc, m, norm_mix[i], wd, mla_qnorm[j], wuq, mla_kvnorm[j],
                                              None, None, latent=False, tm=tm_c)
            ql, ckl = _proj_mla(xl, m, norm_mix[i], wd, mla_qnorm[j], wuq, mla_kvnorm[j],
                                mla_q_tables, mla_k_tables, latent=True, tm=tm_l)
            cache = jnp.concatenate(
                [cache_mla_ckv[:, j], cache_mla_kpe[:, j],
                 jnp.zeros((nd, past, MLA_CK - MLA_KV_LORA - MLA_ROPE), F32)], axis=-1).astype(BF16)
            ck_all = jnp.concatenate([ckl.reshape(nd, sl, MLA_CK), cache], axis=1)
            wq = MLA_HEADS * LANES
            kc, vc = _kv_expand(ckc, wk, wv, tm_c)
            kl, vl = _kv_expand(ck_all.reshape(nd * (sl + past), MLA_CK), wk, wv, 512)
            oc = _attention(qc.reshape(nb, sc, wq), [(kc.reshape(nb, sc, wq), vc.reshape(nb, sc, wq))],
                            gqa=False, steps=1, tq=sc)
            ol = _attention(ql.reshape(nd, sl, wq), [(kl.reshape(nd, sl + past, wq), vl.reshape(nd, sl + past, wq))],
                            gqa=False, steps=MLA_HEADS // 2, tq=tq)
            states["ckv"].append(ckv32.reshape(nb, sc, MLA_KV_LORA))
            states["kpe"].append(kpe32.reshape(nb, sc, MLA_ROPE))
        final = i == DEPTH - 1
        pre = (m, norm_ffn[i], wo.astype(BF16), wr_hi, wr_lo, rbias)
        post = (m, norm_ffn[i], wgu[i], wdn[i], final_norm)
        xc, idx, wts = _moe_pre(xc, oc.reshape(nb * sc, D_MODEL), *pre, latent=False, tm=tm_c)
        xc = _moe(xc, idx, wts, *post, latent=False, final=final)
        xl, idx, wts = _moe_pre(xl, ol.reshape(nd * sl, D_MODEL), *pre, latent=True, tm=tm_l)
        xl = _moe(xl, idx, wts, *post, latent=True, final=final)
    return (xc.reshape(nb, sc, D_MODEL), xl.reshape(nd, sl, D_MODEL),
            jnp.stack(states["swa_k"], axis=1), jnp.stack(states["swa_v"], axis=1),
            jnp.stack(states["qkn_k"], axis=1), jnp.stack(states["qkn_v"], axis=1),
            jnp.stack(states["ckv"], axis=1), jnp.stack(states["kpe"], axis=1))
```

```python
import functools

import numpy as np
import jax
import jax.numpy as jnp
from jax import lax
from jax.experimental import pallas as pl
from jax.experimental.pallas import tpu as pltpu

F32 = jnp.float32
BF16 = jnp.bfloat16

D_MODEL = 1024
DEPTH = 4
GRID_W = 64
N_MIXERS = 3
N_HEADS = 16
N_KV_HEADS = 4
HEAD_DIM = 64
QKV_DIM = (N_HEADS + 2 * N_KV_HEADS) * HEAD_DIM
WINDOW = 128
ROPE_THETA = 10000.0
MLA_HEADS = 16
MLA_Q_LORA = 384
MLA_KV_LORA = 256
MLA_NOPE = 64
MLA_ROPE = 32
MLA_V = 64
N_EXPERTS = 16
N_GROUPS = 4
EXPERTS_PER_GROUP = N_EXPERTS // N_GROUPS
D_EXPERT = 256
EPS = 1e-6

LANES = 128
COND_ROWS = 16
MLA_DOWN = 768
MLA_CK = 384
MOE_BLOCK = 1024
MOE_ROWS = 160
SLOT_BITS = 16
MOE_UNROLL = 8
NEG_BIG = -1e30
VMEM_LIMIT = 56 * 1024 * 1024


def _cparams(sem):
    return pltpu.CompilerParams(dimension_semantics=sem, vmem_limit_bytes=VMEM_LIMIT)


def _sigmoid(x):
    return 1.0 / (1.0 + jnp.exp(-x))


def _rms(x, g):
    return x * lax.rsqrt(jnp.mean(x * x, axis=-1, keepdims=True) + EPS) * g


def _dot(a, b):
    return jnp.dot(a, b, preferred_element_type=F32)


def _dot_nt(a, b):
    return lax.dot_general(a, b, (((1,), (1,)), ((), ())), preferred_element_type=F32)


def _adaln_kernel(cond_ref, w_ref, b_ref, o_ref):
    c = cond_ref[...]
    s = (c * _sigmoid(c)).astype(BF16)
    o_ref[...] = _dot(s, w_ref[...].astype(BF16)) + b_ref[...]


def _adaln(cond, w_mod, b_mod):
    tn = 1536
    n = 6 * D_MODEL
    out = pl.pallas_call(
        _adaln_kernel,
        grid=(DEPTH, n // tn),
        in_specs=[
            pl.BlockSpec((COND_ROWS, D_MODEL), lambda l, j: (0, 0)),
            pl.BlockSpec((None, D_MODEL, tn), lambda l, j: (l, 0, j)),
            pl.BlockSpec((None, 1, tn), lambda l, j: (l, 0, j)),
        ],
        out_specs=pl.BlockSpec((None, COND_ROWS, tn), lambda l, j: (l, 0, j)),
        out_shape=jax.ShapeDtypeStruct((DEPTH, COND_ROWS, n), F32),
        compiler_params=_cparams(("parallel", "parallel")),
        name="adaln",
    )(cond, w_mod, b_mod.reshape(DEPTH, 1, n))
    return out.reshape(DEPTH, COND_ROWS, 6, D_MODEL)


def _mod_spec(rows_per_cond):
    if rows_per_cond == 0:
        return pl.BlockSpec((1, 6, D_MODEL), lambda i: (0, 0, 0))
    return pl.BlockSpec((1, 6, D_MODEL), lambda i: (1 + i // rows_per_cond, 0, 0))


def _rope_tables(seq, head_dim, lane0, period):
    half, quarter = head_dim // 2, head_dim // 4
    pos = np.arange(seq)
    row, col = (pos // GRID_W).astype(np.float64), (pos % GRID_W).astype(np.float64)
    inv = ROPE_THETA ** (-np.arange(quarter, dtype=np.float64) / quarter)
    ang = np.concatenate([row[:, None] * inv, col[:, None] * inv], axis=-1)
    cos = np.ones((seq, LANES))
    s1 = np.zeros((seq, LANES))
    s2 = np.zeros((seq, LANES))
    starts = [lane0] if period == 0 else list(range(lane0, LANES, period))
    for st in starts:
        cos[:, st:st + half] = np.cos(ang)
        cos[:, st + half:st + head_dim] = np.cos(ang)
        s1[:, st:st + half] = -np.sin(ang)
        s2[:, st + half:st + head_dim] = np.sin(ang)
    return tuple(jnp.asarray(t, F32) for t in (cos, s1, s2))


def _rope_chunk(x, cos, s1, s2, half):
    return x * cos + pltpu.roll(x, LANES - half, 1) * s1 + pltpu.roll(x, half, 1) * s2


def _head_layout(kv):
    lo = lax.broadcasted_iota(jnp.int32, (kv.shape[0], LANES), 1) < HEAD_DIM
    zero = jnp.zeros((kv.shape[0], LANES), F32)
    out = []
    for j in range(2):
        ch = kv[:, LANES * j:LANES * (j + 1)]
        ro = pltpu.roll(ch, HEAD_DIM, 1)
        out += [jnp.where(lo, ch, zero), jnp.where(lo, zero, ro), jnp.where(lo, ro, zero), jnp.where(lo, zero, ch)]
    return jnp.concatenate(out, axis=1)


def _proj_gqa_kernel(*refs, qknorm, rope, state):
    it = iter(refs)
    x_ref, mod_ref, g_ref, w_ref = next(it), next(it), next(it), next(it)
    if qknorm:
        qn_ref, kn_ref, gm_ref = next(it), next(it), next(it)
    if rope:
        cos_ref, s1_ref, s2_ref = next(it), next(it), next(it)
    q_ref, kp_ref, vp_ref = next(it), next(it), next(it)
    if state:
        k32_ref, v32_ref = next(it), next(it)

    mod = mod_ref[0]
    h = _rms(x_ref[...], g_ref[...]) * (1.0 + mod[1:2]) + mod[0:1]
    qkv = _dot(h.astype(BF16), w_ref[...])
    nq = N_HEADS * HEAD_DIM
    nk = N_KV_HEADS * HEAD_DIM
    chunks = [qkv[:, LANES * j:LANES * (j + 1)] for j in range((nq + nk) // LANES)]
    if qknorm:
        gains = [qn_ref[...]] * (nq // LANES) + [kn_ref[...]] * (nk // LANES)
        gm = gm_ref[...]
        chunks = [c * lax.rsqrt(_dot((c * c).astype(BF16), gm) + EPS) * g for c, g in zip(chunks, gains)]
    k_state = jnp.concatenate(chunks[nq // LANES:], axis=1)
    if rope:
        cos, s1, s2 = cos_ref[...], s1_ref[...], s2_ref[...]
        chunks = [_rope_chunk(c, cos, s1, s2, HEAD_DIM // 2) for c in chunks]
    q = jnp.concatenate(chunks[:nq // LANES], axis=1) * (HEAD_DIM ** -0.5)
    k = jnp.concatenate(chunks[nq // LANES:], axis=1)
    v = qkv[:, nq + nk:]
    q_ref[...] = q.astype(BF16)
    kp_ref[...] = _head_layout(k).astype(BF16)
    vp_ref[...] = _head_layout(v).astype(BF16)
    if state:
        k32_ref[...] = k_state
        v32_ref[...] = v


def _proj_gqa(x, mods, gain, w, qn, kn, tables, *, latent, tm):
    n = x.shape[0]
    qknorm = qn is not None
    tiles_per_seq = 2048 // tm
    ins = [x, mods, gain.reshape(1, D_MODEL), w]
    specs = [
        pl.BlockSpec((tm, D_MODEL), lambda i: (i, 0)),
        _mod_spec(tiles_per_seq if latent else 0),
        pl.BlockSpec((1, D_MODEL), lambda i: (0, 0)),
        pl.BlockSpec((D_MODEL, QKV_DIM), lambda i: (0, 0)),
    ]
    if qknorm:
        gm = np.kron(np.eye(2), np.full((HEAD_DIM, HEAD_DIM), 1.0 / HEAD_DIM))
        ins += [jnp.tile(qn, 2).reshape(1, LANES), jnp.tile(kn, 2).reshape(1, LANES), jnp.asarray(gm, BF16)]
        specs += [pl.BlockSpec((1, LANES), lambda i: (0, 0))] * 2 + [pl.BlockSpec((LANES, LANES), lambda i: (0, 0))]
    if latent:
        ins += list(tables)
        specs += [pl.BlockSpec((tm, LANES), lambda i: (i % tiles_per_seq, 0))] * 3
    wide = N_HEADS * HEAD_DIM
    outs = [jax.ShapeDtypeStruct((n, wide), BF16)] * 3
    ospecs = [pl.BlockSpec((tm, wide), lambda i: (i, 0))] * 3
    if not latent:
        nk = N_KV_HEADS * HEAD_DIM
        outs += [jax.ShapeDtypeStruct((n, nk), F32)] * 2
        ospecs += [pl.BlockSpec((tm, nk), lambda i: (i, 0))] * 2
    return pl.pallas_call(
        functools.partial(_proj_gqa_kernel, qknorm=qknorm, rope=latent, state=not latent),
        grid=(n // tm,),
        in_specs=specs,
        out_specs=ospecs,
        out_shape=outs,
        compiler_params=_cparams(("parallel",)),
        name="proj_gqa_lat" if latent else "proj_gqa_ctx",
    )(*ins)


def _proj_mla_kernel(*refs, rope, state):
    it = iter(refs)
    x_ref, mod_ref, g_ref, wd_ref, qn_ref, wuq_ref, kvn_ref = (next(it) for _ in range(7))
    if rope:
        qt = [next(it) for _ in range(3)]
        kt = [next(it) for _ in range(3)]
    q_ref, ck_ref = next(it), next(it)
    if state:
        ckv_ref, kpe_ref = next(it), next(it)

    mod = mod_ref[0]
    h = _rms(x_ref[...], g_ref[...]) * (1.0 + mod[1:2]) + mod[0:1]
    d = _dot(h.astype(BF16), wd_ref[...])
    cq = _rms(d[:, :MLA_Q_LORA], qn_ref[...])
    q = _dot(cq.astype(BF16), wuq_ref[...])
    ckv = _rms(d[:, MLA_Q_LORA:MLA_Q_LORA + MLA_KV_LORA], kvn_ref[...])
    kpe = d[:, MLA_Q_LORA + MLA_KV_LORA:]
    kpe_state = kpe
    scale = (MLA_NOPE + MLA_ROPE) ** -0.5
    qs = [q[:, LANES * j:LANES * (j + 1)] for j in range(MLA_HEADS)]
    if rope:
        cos, s1, s2 = (t[...] for t in qt)
        qs = [_rope_chunk(c, cos, s1, s2, MLA_ROPE // 2) for c in qs]
        kpe = _rope_chunk(kpe, *(t[...] for t in kt), MLA_ROPE // 2)
    q_ref[...] = (jnp.concatenate(qs, axis=1) * scale).astype(BF16)
    ck_ref[...] = jnp.concatenate([ckv, kpe], axis=1).astype(BF16)
    if state:
        ckv_ref[...] = ckv
        kpe_ref[...] = kpe_state[:, :MLA_ROPE]


def _proj_mla(x, mods, gain, wd, qn, wuq, kvn, qtables, ktables, *, latent, tm):
    n = x.shape[0]
    tiles_per_seq = 2048 // tm
    wq = MLA_HEADS * LANES
    ins = [x, mods, gain.reshape(1, D_MODEL), wd, qn.reshape(1, MLA_Q_LORA), wuq, kvn.reshape(1, MLA_KV_LORA)]
    specs = [
        pl.BlockSpec((tm, D_MODEL), lambda i: (i, 0)),
        _mod_spec(tiles_per_seq if latent else 0),
        pl.BlockSpec((1, D_MODEL), lambda i: (0, 0)),
        pl.BlockSpec((D_MODEL, MLA_DOWN), lambda i: (0, 0)),
        pl.BlockSpec((1, MLA_Q_LORA), lambda i: (0, 0)),
        pl.BlockSpec((MLA_Q_LORA, wq), lambda i: (0, 0)),
        pl.BlockSpec((1, MLA_KV_LORA), lambda i: (0, 0)),
    ]
    if latent:
        ins += list(qtables) + list(ktables)
        specs += [pl.BlockSpec((tm, LANES), lambda i: (i % tiles_per_seq, 0))] * 6
    outs = [jax.ShapeDtypeStruct((n, wq), BF16), jax.ShapeDtypeStruct((n, MLA_CK), BF16)]
    ospecs = [pl.BlockSpec((tm, wq), lambda i: (i, 0)), pl.BlockSpec((tm, MLA_CK), lambda i: (i, 0))]
    if not latent:
        outs += [jax.ShapeDtypeStruct((n, MLA_KV_LORA), F32), jax.ShapeDtypeStruct((n, MLA_ROPE), F32)]
        ospecs += [pl.BlockSpec((tm, MLA_KV_LORA), lambda i: (i, 0)), pl.BlockSpec((tm, MLA_ROPE), lambda i: (i, 0))]
    return pl.pallas_call(
        functools.partial(_proj_mla_kernel, rope=latent, state=not latent),
        grid=(n // tm,),
        in_specs=specs,
        out_specs=ospecs,
        out_shape=outs,
        compiler_params=_cparams(("parallel",)),
        name="proj_mla_lat" if latent else "proj_mla_ctx",
    )(*ins)


def _kv_expand_kernel(ck_ref, wk_ref, wv_ref, k_ref, v_ref):
    ck = ck_ref[...]
    k_ref[...] = _dot(ck, wk_ref[...]).astype(BF16)
    v_ref[...] = _dot(ck, wv_ref[...]).astype(BF16)


def _kv_expand(ck, wk, wv, tm):
    n = ck.shape[0]
    wide = MLA_HEADS * LANES
    return pl.pallas_call(
        _kv_expand_kernel,
        grid=(n // tm,),
        in_specs=[
            pl.BlockSpec((tm, MLA_CK), lambda i: (i, 0)),
            pl.BlockSpec((MLA_CK, wide), lambda i: (0, 0)),
            pl.BlockSpec((MLA_CK, wide), lambda i: (0, 0)),
        ],
        out_specs=[pl.BlockSpec((tm, wide), lambda i: (i, 0))] * 2,
        out_shape=[jax.ShapeDtypeStruct((n, wide), BF16)] * 2,
        compiler_params=_cparams(("parallel",)),
        name="mla_kv_expand",
    )(ck, wk, wv)


def _attn_kernel(*refs, pairs, n_seg, tq, band, use_sink, heads_per_step):
    it = iter(refs)
    if use_sink:
        sink_ref = next(it)
    q_ref = next(it)
    kv_refs = [(next(it), next(it)) for _ in range(n_seg)]
    o_ref = next(it)

    if band:
        qi = pl.program_id(2)
        seq = kv_refs[0][0].shape[0]
        win = tq + 2 * WINDOW
        start = pl.multiple_of(jnp.clip(qi * tq - WINDOW, 0, seq - win), WINDOW)
        q_pos = qi * tq + lax.broadcasted_iota(jnp.int32, (tq, win), 0)
        k_pos = start + lax.broadcasted_iota(jnp.int32, (tq, win), 1)
        near = jnp.abs(k_pos - q_pos) <= WINDOW

    for pi, (qe, qo, ke, ko, oo) in enumerate(pairs):
        acc = None
        for parity, (qoff, koff) in enumerate(((qe, ke), (qo, ko))):
            qh = q_ref[:, qoff:qoff + LANES]
            scores, values = [], []
            for si, (k_ref, v_ref) in enumerate(kv_refs):
                if band and si == 0:
                    kh = k_ref[pl.ds(start, win), koff:koff + LANES]
                    vh = v_ref[pl.ds(start, win), koff:koff + LANES]
                    s = jnp.where(near, _dot_nt(qh, kh), NEG_BIG)
                else:
                    kh = k_ref[:, koff:koff + LANES]
                    vh = v_ref[:, koff:koff + LANES]
                    s = _dot_nt(qh, kh)
                scores.append(s)
                values.append(vh)
            m = functools.reduce(jnp.maximum, [jnp.max(s, axis=-1, keepdims=True) for s in scores])
            if use_sink:
                sk = sink_ref[pl.program_id(1) * heads_per_step + 2 * pi + parity]
                m = jnp.maximum(m, sk)
            es = [jnp.exp(s - m) for s in scores]
            denom = functools.reduce(jnp.add, [jnp.sum(e, axis=-1, keepdims=True) for e in es])
            if use_sink:
                denom = denom + jnp.exp(sk - m)
            pv = functools.reduce(jnp.add, [_dot(e.astype(BF16), vh) for e, vh in zip(es, values)])
            pv = pv / denom
            acc = pv if acc is None else acc + pv
        o_ref[:, oo:oo + LANES] = acc.astype(o_ref.dtype)


def _attention(q, segs, *, gqa, steps, tq, band=False, sink=None):
    b, sq, wq_total = q.shape
    wk_total = segs[0][0].shape[2]
    wo_total = N_HEADS * HEAD_DIM
    wq, wk, wo = wq_total // steps, wk_total // steps, wo_total // steps
    n_pairs = wo // LANES
    if gqa:
        pairs = [(LANES * p, LANES * p, 2 * LANES * (p // 2), 2 * LANES * (p // 2) + LANES, LANES * p)
                 for p in range(n_pairs)]
    else:
        pairs = [(2 * LANES * p, 2 * LANES * p + LANES, 2 * LANES * p, 2 * LANES * p + LANES, LANES * p)
                 for p in range(n_pairs)]
    ins, specs = [], []
    if sink is not None:
        ins.append(sink)
        specs.append(pl.BlockSpec(memory_space=pltpu.SMEM))
    ins.append(q)
    specs.append(pl.BlockSpec((None, tq, wq), lambda bi, si, qi: (bi, qi, si)))
    for k, v in segs:
        sk = k.shape[1]
        ins += [k, v]
        specs += [pl.BlockSpec((None, sk, wk), lambda bi, si, qi: (bi, 0, si))] * 2
    return pl.pallas_call(
        functools.partial(_attn_kernel, pairs=pairs, n_seg=len(segs), tq=tq, band=band,
                          use_sink=sink is not None, heads_per_step=2 * n_pairs),
        grid=(b, steps, sq // tq),
        in_specs=specs,
        out_specs=pl.BlockSpec((None, tq, wo), lambda bi, si, qi: (bi, qi, si)),
        out_shape=jax.ShapeDtypeStruct((b, sq, wo_total), BF16),
        compiler_params=_cparams(("parallel", "parallel", "parallel")),
        name="attention",
    )(*ins)


def _route(logits_t, bias_col):
    s = _sigmoid(logits_t)
    sb = s + bias_col
    r = [sb[e:e + 1, :] for e in range(N_EXPERTS)]
    sr = [s[e:e + 1, :] for e in range(N_EXPERTS)]
    gscore = []
    for g in range(N_GROUPS):
        a = r[EXPERTS_PER_GROUP * g:EXPERTS_PER_GROUP * (g + 1)]
        best = None
        for i in range(EXPERTS_PER_GROUP):
            for j in range(i + 1, EXPERTS_PER_GROUP):
                p = a[i] + a[j]
                best = p if best is None else jnp.maximum(best, p)
        gscore.append(best)
    gbest, gsel = gscore[0], jnp.zeros_like(gscore[0], dtype=jnp.int32)
    for g in range(1, N_GROUPS):
        better = gscore[g] > gbest
        gsel = jnp.where(better, g, gsel)
        gbest = jnp.where(better, gscore[g], gbest)
    neg = jnp.full_like(r[0], -jnp.inf)
    cand = [jnp.where(gsel == e // EXPERTS_PER_GROUP, r[e], neg) for e in range(N_EXPERTS)]

    def first_argmax(vals):
        best, idx, w = vals[0], jnp.zeros_like(gsel), sr[0]
        for e in range(1, N_EXPERTS):
            better = vals[e] > best
            idx = jnp.where(better, e, idx)
            w = jnp.where(better, sr[e], w)
            best = jnp.where(better, vals[e], best)
        return idx, w

    i1, w1 = first_argmax(cand)
    i2, w2 = first_argmax([jnp.where(i1 == e, neg, cand[e]) for e in range(N_EXPERTS)])
    tot = w1 + w2
    return i1, i2, w1 / tot, w2 / tot


def _moe_pre_kernel(x_ref, o_ref, mod_ref, g_ref, wo_ref, wr_hi_ref, wr_lo_ref, rb_ref, tri_ref,
                    xnew_ref, slot_ref, wts_ref, meta_ref):
    mod = mod_ref[0]
    xn = x_ref[...] + mod[2:3] * _dot(o_ref[...], wo_ref[...])
    xnew_ref[...] = xn
    h = _rms(xn, g_ref[...]) * (1.0 + mod[4:5]) + mod[3:4]
    h_hi = h.astype(BF16)
    h_lo = (h - h_hi.astype(F32)).astype(BF16)
    logits_t = _dot_nt(wr_hi_ref[...], h_hi) + _dot_nt(wr_hi_ref[...], h_lo) + _dot_nt(wr_lo_ref[...], h_hi)
    i1, i2, g1, g2 = _route(logits_t, rb_ref[...])
    pad = jnp.zeros((LANES - 2, g1.shape[1]), F32)
    wts_ref[...] = jnp.concatenate([g1, g2, pad], axis=0).T

    rows = lax.broadcasted_iota(jnp.int32, logits_t.shape, 0)
    member = jnp.where((rows == i1) | (rows == i2), 1.0, 0.0)
    rank = _dot(member.astype(BF16), tri_ref[...])
    count = jnp.sum(member, axis=1, keepdims=True)
    lane = lax.broadcasted_iota(jnp.int32, (1, LANES), 1)
    start = jnp.zeros((1, 1), F32)
    slot1 = jnp.zeros_like(g1)
    slot2 = jnp.zeros_like(g1)
    starts = jnp.zeros((1, LANES), F32)
    chunks = jnp.zeros((1, LANES), F32)
    for e in range(N_EXPERTS):
        pos = start + rank[e:e + 1, :]
        slot1 = jnp.where(i1 == e, pos, slot1)
        slot2 = jnp.where(i2 == e, pos, slot2)
        cnt = count[e:e + 1, :]
        starts = jnp.where(lane == e, start, starts)
        chunks = jnp.where(lane == e, jnp.floor((cnt + (MOE_ROWS - 1)) * (1.0 / MOE_ROWS)), chunks)
        start = start + cnt
    sub = D_MODEL // LANES
    slot_ref[...] = slot1.astype(jnp.int32) * sub + slot2.astype(jnp.int32) * (sub << SLOT_BITS)
    meta_ref[...] = jnp.concatenate([starts, chunks], axis=0).astype(jnp.int32)


def _moe_pre(x, o, mods, gain, wo, wr_hi, wr_lo, rbias, *, latent):
    n = x.shape[0]
    tm = MOE_BLOCK
    const = lambda i: (0, 0)
    tri = jnp.asarray(np.triu(np.ones((tm, tm), np.float32), k=1), BF16)
    return pl.pallas_call(
        _moe_pre_kernel,
        grid=(n // tm,),
        in_specs=[
            pl.BlockSpec((tm, D_MODEL), lambda i: (i, 0)),
            pl.BlockSpec((tm, D_MODEL), lambda i: (i, 0)),
            _mod_spec((2048 // tm) if latent else 0),
            pl.BlockSpec((1, D_MODEL), const),
            pl.BlockSpec((D_MODEL, D_MODEL), const),
            pl.BlockSpec((LANES, D_MODEL), const),
            pl.BlockSpec((LANES, D_MODEL), const),
            pl.BlockSpec((LANES, 1), const),
            pl.BlockSpec((tm, tm), const),
        ],
        out_specs=[
            pl.BlockSpec((tm, D_MODEL), lambda i: (i, 0)),
            pl.BlockSpec((None, 1, tm), lambda i: (i, 0, 0)),
            pl.BlockSpec((tm, LANES), lambda i: (i, 0)),
            pl.BlockSpec((None, 2, LANES), lambda i: (i, 0, 0)),
        ],
        out_shape=[
            jax.ShapeDtypeStruct((n, D_MODEL), F32),
            jax.ShapeDtypeStruct((n // tm, 1, tm), jnp.int32),
            jax.ShapeDtypeStruct((n, LANES), F32),
            jax.ShapeDtypeStruct((n // tm, 2, LANES), jnp.int32),
        ],
        compiler_params=_cparams(("parallel",)),
        name="moe_pre_lat" if latent else "moe_pre_ctx",
    )(x, o, mods, gain.reshape(1, D_MODEL), wo, wr_hi, wr_lo, rbias, tri)


def _moe_kernel(slot_ref, start_ref, chunks_ref,
                xnew_ref, wts_ref, mod_ref, g_ref, wgu_ref, wd_ref, fin_ref, out_ref,
                tok_ref, tok2_ref, xs_ref, y_ref, *, tb, final):
    mask = (1 << SLOT_BITS) - 1
    b, e = pl.program_id(0), pl.program_id(1)
    tok0 = b * tb
    sub = D_MODEL // LANES

    @pl.when(e == 0)
    def _():
        @pl.when(b == 0)
        def _():
            xs_ref[2 * tb * sub:, :] = jnp.zeros((MOE_ROWS * sub, LANES), F32)

        mod = mod_ref[0]
        h = _rms(xnew_ref[...], g_ref[...]) * (1.0 + mod[4:5]) + mod[3:4]
        for s in range(sub):
            tok_ref[pl.ds(s, tb, stride=sub), :] = h[:, LANES * s:LANES * (s + 1)]

        def dispatch(i, carry):
            for k in range(MOE_UNROLL):
                t = i * MOE_UNROLL + k
                row = tok_ref[pl.ds(pl.multiple_of(t * sub, sub), sub), :]
                packed = slot_ref[tok0 + t]
                xs_ref[pl.ds(pl.multiple_of(packed & mask, sub), sub), :] = row
                xs_ref[pl.ds(pl.multiple_of(packed >> SLOT_BITS, sub), sub), :] = row
            return carry
        lax.fori_loop(0, tb // MOE_UNROLL, dispatch, 0)

    start = start_ref[b * N_EXPERTS + e]

    def chunk(c, carry):
        base = pl.multiple_of((start + c * MOE_ROWS) * sub, sub)
        rows_in = xs_ref.at[pl.ds(base, MOE_ROWS * sub)]
        lhs = jnp.concatenate([rows_in[pl.ds(s, MOE_ROWS, stride=sub), :] for s in range(sub)], axis=1)
        au = _dot(lhs.astype(BF16), wgu_ref[...])
        a, u = au[:, :D_EXPERT], au[:, D_EXPERT:]
        y = _dot((a * _sigmoid(a) * u).astype(BF16), wd_ref[...])
        rows_out = y_ref.at[pl.ds(base, MOE_ROWS * sub)]
        for s in range(sub):
            rows_out[pl.ds(s, MOE_ROWS, stride=sub), :] = y[:, LANES * s:LANES * (s + 1)]
        return carry
    lax.fori_loop(0, chunks_ref[b * N_EXPERTS + e], chunk, 0)

    @pl.when(e == N_EXPERTS - 1)
    def _():
        def combine(i, carry):
            for k in range(MOE_UNROLL):
                t = i * MOE_UNROLL + k
                packed = slot_ref[tok0 + t]
                dst = pl.ds(pl.multiple_of(t * sub, sub), sub)
                tok_ref[dst, :] = y_ref[pl.ds(pl.multiple_of(packed & mask, sub), sub), :]
                tok2_ref[dst, :] = y_ref[pl.ds(pl.multiple_of(packed >> SLOT_BITS, sub), sub), :]
            return carry
        lax.fori_loop(0, tb // MOE_UNROLL, combine, 0)
        first = jnp.concatenate([tok_ref[pl.ds(s, tb, stride=sub), :] for s in range(sub)], axis=1)
        second = jnp.concatenate([tok2_ref[pl.ds(s, tb, stride=sub), :] for s in range(sub)], axis=1)
        wts = wts_ref[...]
        moe = wts[:, 0:1] * first + wts[:, 1:2] * second
        out = xnew_ref[...] + mod_ref[0][5:6] * moe
        if final:
            out = _rms(out, fin_ref[...])
        out_ref[...] = out


def _moe(xnew, slots, wts, meta, mods, gain, wgu, wd, fin, *, latent, final):
    n = xnew.shape[0]
    tb = MOE_BLOCK
    cap = 2 * tb + MOE_ROWS
    sub = D_MODEL // LANES
    assert cap * sub < (1 << SLOT_BITS)
    const = lambda i, e: (0, 0)
    rows_per_cond = (2048 // tb) if latent else 0
    if rows_per_cond == 0:
        mod_spec = pl.BlockSpec((1, 6, D_MODEL), lambda i, e: (0, 0, 0))
    else:
        mod_spec = pl.BlockSpec((1, 6, D_MODEL), lambda i, e: (1 + i // rows_per_cond, 0, 0))
    smem = pl.BlockSpec(memory_space=pltpu.SMEM)
    return pl.pallas_call(
        functools.partial(_moe_kernel, tb=tb, final=final),
        grid=(n // tb, N_EXPERTS),
        in_specs=[smem] * 3 + [
            pl.BlockSpec((tb, D_MODEL), lambda i, e: (i, 0)),
            pl.BlockSpec((tb, LANES), lambda i, e: (i, 0)),
            mod_spec,
            pl.BlockSpec((1, D_MODEL), const),
            pl.BlockSpec((None, D_MODEL, 2 * D_EXPERT), lambda i, e: (e, 0, 0)),
            pl.BlockSpec((None, D_EXPERT, D_MODEL), lambda i, e: (e, 0, 0)),
            pl.BlockSpec((1, D_MODEL), const),
        ],
        out_specs=pl.BlockSpec((tb, D_MODEL), lambda i, e: (i, 0)),
        out_shape=jax.ShapeDtypeStruct((n, D_MODEL), F32),
        scratch_shapes=[
            pltpu.VMEM((tb * sub, LANES), F32),
            pltpu.VMEM((tb * sub, LANES), F32),
            pltpu.VMEM((cap * sub, LANES), F32),
            pltpu.VMEM((cap * sub, LANES), F32),
        ],
        compiler_params=_cparams(("arbitrary", "arbitrary")),
        name="moe_lat" if latent else "moe_ctx",
    )(slots.reshape(n), meta[:, 0, :N_EXPERTS].reshape(-1), meta[:, 1, :N_EXPERTS].reshape(-1),
      xnew, wts, mods, gain.reshape(1, D_MODEL), wgu, wd, fin.reshape(1, D_MODEL))


def _expert_weights_kernel(wg_ref, wu_ref, wd_ref, wgu_ref, wdn_ref):
    wgu_ref[:, :D_EXPERT] = wg_ref[...].astype(BF16)
    wgu_ref[:, D_EXPERT:] = wu_ref[...].astype(BF16)
    wdn_ref[...] = wd_ref[...].astype(BF16)


def _expert_weights(w_gate, w_up, w_down):
    up = pl.BlockSpec((None, None, D_MODEL, D_EXPERT), lambda l, e: (l, e, 0, 0))
    down = pl.BlockSpec((None, None, D_EXPERT, D_MODEL), lambda l, e: (l, e, 0, 0))
    return pl.pallas_call(
        _expert_weights_kernel,
        grid=(DEPTH, N_EXPERTS),
        in_specs=[up, up, down],
        out_specs=[pl.BlockSpec((None, None, D_MODEL, 2 * D_EXPERT), lambda l, e: (l, e, 0, 0)), down],
        out_shape=[jax.ShapeDtypeStruct((DEPTH, N_EXPERTS, D_MODEL, 2 * D_EXPERT), BF16),
                   jax.ShapeDtypeStruct((DEPTH, N_EXPERTS, D_EXPERT, D_MODEL), BF16)],
        compiler_params=_cparams(("parallel", "parallel")),
        name="expert_weights",
    )(w_gate, w_up, w_down)


def _cache_head_layout(c):
    z = jnp.zeros_like(c)
    even = jnp.concatenate([c, z], axis=-1)
    odd = jnp.concatenate([z, c], axis=-1)
    return jnp.stack([even, odd], axis=3).reshape(c.shape[0], c.shape[1], 2 * N_KV_HEADS * LANES).astype(BF16)


def _mla_weights(wdq, wdkv, wuq, wukv):
    wd = jnp.concatenate([wdq, wdkv, jnp.zeros((D_MODEL, MLA_DOWN - MLA_Q_LORA - MLA_KV_LORA - MLA_ROPE), F32)], axis=1)
    dk = MLA_NOPE + MLA_ROPE
    wuq_p = jnp.pad(wuq.reshape(MLA_Q_LORA, MLA_HEADS, dk), ((0, 0), (0, 0), (0, LANES - dk)))
    wuq_p = wuq_p.reshape(MLA_Q_LORA, MLA_HEADS * LANES)
    kv = wukv.reshape(MLA_KV_LORA, MLA_HEADS, MLA_NOPE + MLA_V)
    k_nope, v = kv[..., :MLA_NOPE], kv[..., MLA_NOPE:]
    wk_top = jnp.pad(k_nope, ((0, 0), (0, 0), (0, LANES - MLA_NOPE))).reshape(MLA_KV_LORA, MLA_HEADS * LANES)
    place = np.zeros((MLA_CK - MLA_KV_LORA, MLA_HEADS, LANES), np.float32)
    for r in range(MLA_ROPE):
        place[r, :, MLA_NOPE + r] = 1.0
    wk = jnp.concatenate([wk_top, jnp.asarray(place.reshape(MLA_CK - MLA_KV_LORA, MLA_HEADS * LANES))], axis=0)
    z = jnp.zeros_like(v)
    v_even = jnp.concatenate([v, z], axis=-1)
    v_odd = jnp.concatenate([z, v], axis=-1)
    parity = (jnp.arange(MLA_HEADS) % 2 == 0)[None, :, None]
    wv_top = jnp.where(parity, v_even, v_odd).reshape(MLA_KV_LORA, MLA_HEADS * LANES)
    wv = jnp.concatenate([wv_top, jnp.zeros((MLA_CK - MLA_KV_LORA, MLA_HEADS * LANES), F32)], axis=0)
    return wd.astype(BF16), wuq_p.astype(BF16), wk.astype(BF16), wv.astype(BF16)


def kernel(x_prompt, x_sample, cache_swa_k, cache_swa_v, cache_qkn_k, cache_qkn_v, cache_mla_ckv, cache_mla_kpe, c, c_ctx, w_mod, b_mod, norm_mix, norm_ffn, swa_wqkv, swa_wo, swa_sink, qkn_wqkv, qkn_wo, qkn_qnorm, qkn_knorm, mla_wdq, mla_qnorm, mla_wuq, mla_wdkv, mla_kvnorm, mla_wukv, mla_wo, w_router, router_bias, moe_w_gate, moe_w_up, moe_w_down, final_norm):
    nb, sc, _ = x_prompt.shape
    nd, sl, _ = x_sample.shape
    past = cache_swa_k.shape[2]
    cond = jnp.zeros((COND_ROWS, D_MODEL), F32).at[0].set(c_ctx).at[1:1 + nd].set(c)
    mods = _adaln(cond, w_mod, b_mod)

    gqa_tables = _rope_tables(sl, HEAD_DIM, 0, HEAD_DIM)
    mla_q_tables = _rope_tables(sl, MLA_ROPE, MLA_NOPE, 0)
    mla_k_tables = _rope_tables(sl, MLA_ROPE, 0, 0)

    wr_t = jnp.zeros((LANES, D_MODEL), F32).at[:N_EXPERTS].set(w_router.T)
    wr_hi = wr_t.astype(BF16)
    wr_lo = (wr_t - wr_hi.astype(F32)).astype(BF16)
    rbias = jnp.zeros((LANES, 1), F32).at[:N_EXPERTS, 0].set(router_bias)
    wgu, wdn = _expert_weights(moe_w_gate, moe_w_up, moe_w_down)

    xc = x_prompt.reshape(nb * sc, D_MODEL)
    xl = x_sample.reshape(nd * sl, D_MODEL)
    tm_c, tm_l, tq = 512, 512, 256
    states = {k: [] for k in ("swa_k", "swa_v", "qkn_k", "qkn_v", "ckv", "kpe")}
    for i in range(DEPTH):
        kind, j = i % N_MIXERS, i // N_MIXERS
        m = mods[i]
        if kind in (0, 1):
            if kind == 0:
                w, wo, qn, kn = swa_wqkv[j], swa_wo[j], None, None
                ck, cv, sink = cache_swa_k[:, j], cache_swa_v[:, j], swa_sink[j]
            else:
                w, wo, qn, kn = qkn_wqkv[j], qkn_wo[j], qkn_qnorm[j], qkn_knorm[j]
                ck, cv, sink = cache_qkn_k[:, j], cache_qkn_v[:, j], None
            w = w.astype(BF16)
            qc, kc, vc, k32, v32 = _proj_gqa(xc, m, norm_mix[i], w, qn, kn, None, latent=False, tm=tm_c)
            ql, kl, vl = _proj_gqa(xl, m, norm_mix[i], w, qn, kn, gqa_tables, latent=True, tm=tm_l)
            wide = N_HEADS * HEAD_DIM
            oc = _attention(qc.reshape(nb, sc, wide), [(kc.reshape(nb, sc, wide), vc.reshape(nb, sc, wide))],
                            gqa=True, steps=1, tq=sc, sink=sink)
            ol = _attention(ql.reshape(nd, sl, wide),
                            [(kl.reshape(nd, sl, wide), vl.reshape(nd, sl, wide)),
                             (_cache_head_layout(ck), _cache_head_layout(cv))],
                            gqa=True, steps=N_KV_HEADS, tq=tq, band=(kind == 0), sink=sink)
            names = ("swa_k", "swa_v") if kind == 0 else ("qkn_k", "qkn_v")
            states[names[0]].append(k32.reshape(nb, sc, N_KV_HEADS, HEAD_DIM))
            states[names[1]].append(v32.reshape(nb, sc, N_KV_HEADS, HEAD_DIM))
        else:
            wd, wuq, wk, wv = _mla_weights(mla_wdq[j], mla_wdkv[j], mla_wuq[j], mla_wukv[j])
            wo = mla_wo[j]
            qc, ckc, ckv32, kpe32 = _proj_mla(xc, m, norm_mix[i], wd, mla_qnorm[j], wuq, mla_kvnorm[j],
                                              None, None, latent=False, tm=tm_c)
            ql, ckl = _proj_mla(xl, m, norm_mix[i], wd, mla_qnorm[j], wuq, mla_kvnorm[j],
                                mla_q_tables, mla_k_tables, latent=True, tm=tm_l)
            cache = jnp.concatenate(
                [cache_mla_ckv[:, j], cache_mla_kpe[:, j],
                 jnp.zeros((nd, past, MLA_CK - MLA_KV_LORA - MLA_ROPE), F32)], axis=-1).astype(BF16)
            ck_all = jnp.concatenate([ckl.reshape(nd, sl, MLA_CK), cache], axis=1)
            wq = MLA_HEADS * LANES
            kc, vc = _kv_expand(ckc, wk, wv, tm_c)
            kl, vl = _kv_expand(ck_all.reshape(nd * (sl + past), MLA_CK), wk, wv, 512)
            oc = _attention(qc.reshape(nb, sc, wq), [(kc.reshape(nb, sc, wq), vc.reshape(nb, sc, wq))],
                            gqa=False, steps=1, tq=sc)
            ol = _attention(ql.reshape(nd, sl, wq), [(kl.reshape(nd, sl + past, wq), vl.reshape(nd, sl + past, wq))],
                            gqa=False, steps=MLA_HEADS // 2, tq=tq)
            states["ckv"].append(ckv32.reshape(nb, sc, MLA_KV_LORA))
            states["kpe"].append(kpe32.reshape(nb, sc, MLA_ROPE))
        final = i == DEPTH - 1
        pre = (m, norm_ffn[i], wo.astype(BF16), wr_hi, wr_lo, rbias)
        post = (m, norm_ffn[i], wgu[i], wdn[i], final_norm)
        xc, slots, wts, meta = _moe_pre(xc, oc.reshape(nb * sc, D_MODEL), *pre, latent=False)
        xc = _moe(xc, slots, wts, meta, *post, latent=False, final=final)
        xl, slots, wts, meta = _moe_pre(xl, ol.reshape(nd * sl, D_MODEL), *pre, latent=True)
        xl = _moe(xl, slots, wts, meta, *post, latent=True, final=final)
    return (xc.reshape(nb, sc, D_MODEL), xl.reshape(nd, sl, D_MODEL),
            jnp.stack(states["swa_k"], axis=1), jnp.stack(states["swa_v"], axis=1),
            jnp.stack(states["qkn_k"], axis=1), jnp.stack(states["qkn_v"], axis=1),
            jnp.stack(states["ckv"], axis=1), jnp.stack(states["kpe"], axis=1))
```

```python
import functools

import numpy as np
import jax
import jax.numpy as jnp
from jax import lax
from jax.experimental import pallas as pl
from jax.experimental.pallas import tpu as pltpu

F32 = jnp.float32
BF16 = jnp.bfloat16

D_MODEL = 1024
DEPTH = 4
GRID_W = 64
N_MIXERS = 3
N_HEADS = 16
N_KV_HEADS = 4
HEAD_DIM = 64
QKV_DIM = (N_HEADS + 2 * N_KV_HEADS) * HEAD_DIM
WINDOW = 128
ROPE_THETA = 10000.0
MLA_HEADS = 16
MLA_Q_LORA = 384
MLA_KV_LORA = 256
MLA_NOPE = 64
MLA_ROPE = 32
MLA_V = 64
N_EXPERTS = 16
N_GROUPS = 4
EXPERTS_PER_GROUP = N_EXPERTS // N_GROUPS
D_EXPERT = 256
EPS = 1e-6

LANES = 128
COND_ROWS = 16
MLA_DOWN = 768
MLA_CK = 384
PROJ_TOKENS = 512
ATTN_Q = 512
ATTN_Q_MLA = 256
ATTN_Q_BAND = 256
MOE_BLOCK = 1024
MOE_ROWS = 256
MOE_STEP_EXPERTS = 2
SLOT_BITS = 16
MOE_UNROLL = 32
NEG_BIG = -1e30
MXU_DENOM_MIN_KEYS = 1024
LOG2_E = 1.4426950408889634
VMEM_LIMIT = 56 * 1024 * 1024


def _cparams(sem):
    return pltpu.CompilerParams(dimension_semantics=sem, vmem_limit_bytes=VMEM_LIMIT)


def _sigmoid(x):
    return 1.0 / (1.0 + jnp.exp(-x))


def _rms(x, g):
    return x * lax.rsqrt(jnp.mean(x * x, axis=-1, keepdims=True) + EPS) * g


def _modulate(x, g, shift, scale):
    return x * lax.rsqrt(jnp.mean(x * x, axis=-1, keepdims=True) + EPS) * (g * (1.0 + scale)) + shift


def _dot(a, b):
    return jnp.dot(a, b, preferred_element_type=F32)


def _dot_nt(a, b):
    return lax.dot_general(a, b, (((1,), (1,)), ((), ())), preferred_element_type=F32)


def _adaln_kernel(cond_ref, w_ref, b_ref, o_ref):
    c = cond_ref[...]
    s = (c * _sigmoid(c)).astype(BF16)
    o_ref[...] = _dot(s, w_ref[...].astype(BF16)) + b_ref[...]


def _adaln(cond, w_mod, b_mod):
    tn = 1536
    n = 6 * D_MODEL
    out = pl.pallas_call(
        _adaln_kernel,
        grid=(DEPTH, n // tn),
        in_specs=[
            pl.BlockSpec((COND_ROWS, D_MODEL), lambda l, j: (0, 0)),
            pl.BlockSpec((None, D_MODEL, tn), lambda l, j: (l, 0, j)),
            pl.BlockSpec((None, 1, tn), lambda l, j: (l, 0, j)),
        ],
        out_specs=pl.BlockSpec((None, COND_ROWS, tn), lambda l, j: (l, 0, j)),
        out_shape=jax.ShapeDtypeStruct((DEPTH, COND_ROWS, n), F32),
        compiler_params=_cparams(("parallel", "parallel")),
        name="adaln",
    )(cond, w_mod, b_mod.reshape(DEPTH, 1, n))
    return out.reshape(DEPTH, COND_ROWS, 6, D_MODEL)


def _mod_spec(rows_per_cond):
    if rows_per_cond == 0:
        return pl.BlockSpec((1, 6, D_MODEL), lambda i: (0, 0, 0))
    return pl.BlockSpec((1, 6, D_MODEL), lambda i: (1 + i // rows_per_cond, 0, 0))


def _rope_tables(seq, head_dim, lane0, period):
    half, quarter = head_dim // 2, head_dim // 4
    pos = np.arange(seq)
    row, col = (pos // GRID_W).astype(np.float64), (pos % GRID_W).astype(np.float64)
    inv = ROPE_THETA ** (-np.arange(quarter, dtype=np.float64) / quarter)
    ang = np.concatenate([row[:, None] * inv, col[:, None] * inv], axis=-1)
    cos = np.ones((seq, LANES))
    s1 = np.zeros((seq, LANES))
    s2 = np.zeros((seq, LANES))
    starts = [lane0] if period == 0 else list(range(lane0, LANES, period))
    for st in starts:
        cos[:, st:st + half] = np.cos(ang)
        cos[:, st + half:st + head_dim] = np.cos(ang)
        s1[:, st:st + half] = -np.sin(ang)
        s2[:, st + half:st + head_dim] = np.sin(ang)
    return tuple(jnp.asarray(t, F32) for t in (cos, s1, s2))


def _rope_chunk(x, cos, s1, s2, half):
    return x * cos + pltpu.roll(x, LANES - half, 1) * s1 + pltpu.roll(x, half, 1) * s2


def _head_layout(kv, fill):
    lo = lax.broadcasted_iota(jnp.int32, (kv.shape[0], LANES), 1) < HEAD_DIM
    zero = jnp.full((kv.shape[0], LANES), fill, F32)
    out = []
    for j in range(2):
        ch = kv[:, LANES * j:LANES * (j + 1)]
        ro = pltpu.roll(ch, HEAD_DIM, 1)
        out += [jnp.where(lo, ch, zero), jnp.where(lo, zero, ro), jnp.where(lo, ro, zero), jnp.where(lo, zero, ch)]
    return jnp.concatenate(out, axis=1)


def _proj_gqa_kernel(*refs, qknorm, rope, state):
    it = iter(refs)
    x_ref, mod_ref, g_ref, w_ref = next(it), next(it), next(it), next(it)
    if qknorm:
        qn_ref, kn_ref, gm_ref = next(it), next(it), next(it)
    if rope:
        cos_ref, s1_ref, s2_ref = next(it), next(it), next(it)
    q_ref, kp_ref, vp_ref = next(it), next(it), next(it)
    if state:
        k32_ref, v32_ref = next(it), next(it)

    mod = mod_ref[0]
    h = _modulate(x_ref[...], g_ref[...], mod[0:1], mod[1:2])
    qkv = _dot(h.astype(BF16), w_ref[...])
    nq = N_HEADS * HEAD_DIM
    nk = N_KV_HEADS * HEAD_DIM
    chunks = [qkv[:, LANES * j:LANES * (j + 1)] for j in range((nq + nk) // LANES)]
    if qknorm:
        gains = [qn_ref[...]] * (nq // LANES) + [kn_ref[...]] * (nk // LANES)
        gm = gm_ref[...]
        chunks = [c * lax.rsqrt(_dot((c * c).astype(BF16), gm) + EPS) * g for c, g in zip(chunks, gains)]
    k_state = jnp.concatenate(chunks[nq // LANES:], axis=1)
    if rope:
        cos, s1, s2 = cos_ref[...], s1_ref[...], s2_ref[...]
        chunks = [_rope_chunk(c, cos, s1, s2, HEAD_DIM // 2) for c in chunks]
    q = jnp.concatenate(chunks[:nq // LANES], axis=1) * (HEAD_DIM ** -0.5 * LOG2_E)
    k = jnp.concatenate(chunks[nq // LANES:], axis=1)
    v = qkv[:, nq + nk:]
    q_ref[...] = q.astype(BF16)
    kp_ref[...] = _head_layout(k, 0.0).astype(BF16)
    vp_ref[...] = _head_layout(v, 1.0).astype(BF16)
    if state:
        k32_ref[...] = k_state
        v32_ref[...] = v


def _proj_gqa(x, mods, gain, w, qn, kn, tables, *, latent, tm):
    n = x.shape[0]
    qknorm = qn is not None
    tiles_per_seq = 2048 // tm
    ins = [x, mods, gain.reshape(1, D_MODEL), w]
    specs = [
        pl.BlockSpec((tm, D_MODEL), lambda i: (i, 0)),
        _mod_spec(tiles_per_seq if latent else 0),
        pl.BlockSpec((1, D_MODEL), lambda i: (0, 0)),
        pl.BlockSpec((D_MODEL, QKV_DIM), lambda i: (0, 0)),
    ]
    if qknorm:
        gm = np.kron(np.eye(2), np.full((HEAD_DIM, HEAD_DIM), 1.0 / HEAD_DIM))
        ins += [jnp.tile(qn, 2).reshape(1, LANES), jnp.tile(kn, 2).reshape(1, LANES), jnp.asarray(gm, BF16)]
        specs += [pl.BlockSpec((1, LANES), lambda i: (0, 0))] * 2 + [pl.BlockSpec((LANES, LANES), lambda i: (0, 0))]
    if latent:
        ins += list(tables)
        specs += [pl.BlockSpec((tm, LANES), lambda i: (i % tiles_per_seq, 0))] * 3
    wide = N_HEADS * HEAD_DIM
    outs = [jax.ShapeDtypeStruct((n, wide), BF16)] * 3
    ospecs = [pl.BlockSpec((tm, wide), lambda i: (i, 0))] * 3
    if not latent:
        nk = N_KV_HEADS * HEAD_DIM
        outs += [jax.ShapeDtypeStruct((n, nk), F32)] * 2
        ospecs += [pl.BlockSpec((tm, nk), lambda i: (i, 0))] * 2
    return pl.pallas_call(
        functools.partial(_proj_gqa_kernel, qknorm=qknorm, rope=latent, state=not latent),
        grid=(n // tm,),
        in_specs=specs,
        out_specs=ospecs,
        out_shape=outs,
        compiler_params=_cparams(("parallel",)),
        name="proj_gqa_lat" if latent else "proj_gqa_ctx",
    )(*ins)


def _proj_mla_kernel(*refs, rope, state):
    it = iter(refs)
    x_ref, mod_ref, g_ref, wd_ref, qn_ref, wuq_ref, kvn_ref = (next(it) for _ in range(7))
    if rope:
        qt = [next(it) for _ in range(3)]
        kt = [next(it) for _ in range(3)]
    q_ref, ck_ref = next(it), next(it)
    if state:
        ckv_ref, kpe_ref = next(it), next(it)

    mod = mod_ref[0]
    h = _modulate(x_ref[...], g_ref[...], mod[0:1], mod[1:2])
    d = _dot(h.astype(BF16), wd_ref[...])
    cq = _rms(d[:, :MLA_Q_LORA], qn_ref[...])
    q = _dot(cq.astype(BF16), wuq_ref[...])
    ckv = _rms(d[:, MLA_Q_LORA:MLA_Q_LORA + MLA_KV_LORA], kvn_ref[...])
    kpe = d[:, MLA_Q_LORA + MLA_KV_LORA:]
    kpe_state = kpe
    scale = (MLA_NOPE + MLA_ROPE) ** -0.5 * LOG2_E
    qs = [q[:, LANES * j:LANES * (j + 1)] for j in range(MLA_HEADS)]
    if rope:
        cos, s1, s2 = (t[...] for t in qt)
        qs = [_rope_chunk(c, cos, s1, s2, MLA_ROPE // 2) for c in qs]
        kpe = _rope_chunk(kpe, *(t[...] for t in kt), MLA_ROPE // 2)
    q_ref[...] = (jnp.concatenate(qs, axis=1) * scale).astype(BF16)
    ones = lax.broadcasted_iota(jnp.int32, kpe.shape, 1) == MLA_ROPE
    ck_ref[...] = jnp.concatenate([ckv, jnp.where(ones, 1.0, kpe)], axis=1).astype(BF16)
    if state:
        ckv_ref[...] = ckv
        kpe_ref[...] = kpe_state[:, :MLA_ROPE]


def _proj_mla(x, mods, gain, wd, qn, wuq, kvn, qtables, ktables, *, latent, tm):
    n = x.shape[0]
    tiles_per_seq = 2048 // tm
    wq = MLA_HEADS * LANES
    ins = [x, mods, gain.reshape(1, D_MODEL), wd, qn.reshape(1, MLA_Q_LORA), wuq, kvn.reshape(1, MLA_KV_LORA)]
    specs = [
        pl.BlockSpec((tm, D_MODEL), lambda i: (i, 0)),
        _mod_spec(tiles_per_seq if latent else 0),
        pl.BlockSpec((1, D_MODEL), lambda i: (0, 0)),
        pl.BlockSpec((D_MODEL, MLA_DOWN), lambda i: (0, 0)),
        pl.BlockSpec((1, MLA_Q_LORA), lambda i: (0, 0)),
        pl.BlockSpec((MLA_Q_LORA, wq), lambda i: (0, 0)),
        pl.BlockSpec((1, MLA_KV_LORA), lambda i: (0, 0)),
    ]
    if latent:
        ins += list(qtables) + list(ktables)
        specs += [pl.BlockSpec((tm, LANES), lambda i: (i % tiles_per_seq, 0))] * 6
    outs = [jax.ShapeDtypeStruct((n, wq), BF16), jax.ShapeDtypeStruct((n, MLA_CK), BF16)]
    ospecs = [pl.BlockSpec((tm, wq), lambda i: (i, 0)), pl.BlockSpec((tm, MLA_CK), lambda i: (i, 0))]
    if not latent:
        outs += [jax.ShapeDtypeStruct((n, MLA_KV_LORA), F32), jax.ShapeDtypeStruct((n, MLA_ROPE), F32)]
        ospecs += [pl.BlockSpec((tm, MLA_KV_LORA), lambda i: (i, 0)), pl.BlockSpec((tm, MLA_ROPE), lambda i: (i, 0))]
    return pl.pallas_call(
        functools.partial(_proj_mla_kernel, rope=latent, state=not latent),
        grid=(n // tm,),
        in_specs=specs,
        out_specs=ospecs,
        out_shape=outs,
        compiler_params=_cparams(("parallel",)),
        name="proj_mla_lat" if latent else "proj_mla_ctx",
    )(*ins)


def _kv_expand_kernel(ck_ref, wk_ref, wv_ref, k_ref, v_ref):
    ck = ck_ref[...]
    k_ref[...] = _dot(ck, wk_ref[...]).astype(BF16)
    v_ref[...] = _dot(ck, wv_ref[...]).astype(BF16)


def _kv_expand(ck, wk, wv, tm):
    n = ck.shape[0]
    wide = MLA_HEADS * LANES
    return pl.pallas_call(
        _kv_expand_kernel,
        grid=(n // tm,),
        in_specs=[
            pl.BlockSpec((tm, MLA_CK), lambda i: (i, 0)),
            pl.BlockSpec((MLA_CK, wide), lambda i: (0, 0)),
            pl.BlockSpec((MLA_CK, wide), lambda i: (0, 0)),
        ],
        out_specs=[pl.BlockSpec((tm, wide), lambda i: (i, 0))] * 2,
        out_shape=[jax.ShapeDtypeStruct((n, wide), BF16)] * 2,
        compiler_params=_cparams(("parallel",)),
        name="mla_kv_expand",
    )(ck, wk, wv)


def _attn_kernel(*refs, pairs, n_seg, tq, band, use_sink, heads_per_step, mxu_denom):
    it = iter(refs)
    if use_sink:
        sink_ref = next(it)
    if band:
        bias_ref = next(it)
    q_ref = next(it)
    kv_refs = [(next(it), next(it)) for _ in range(n_seg)]
    o_ref = next(it)

    if band:
        qi = pl.program_id(2)
        seq = kv_refs[0][0].shape[0]
        win = tq + 2 * WINDOW
        start = pl.multiple_of(jnp.clip(qi * tq - WINDOW, 0, seq - win), WINDOW)

    low_half = lax.broadcasted_iota(jnp.int32, (tq, LANES), 1) < HEAD_DIM
    for pi, (qe, qo, ke, ko, oo) in enumerate(pairs):
        halves = []
        for parity, (qoff, koff) in enumerate(((qe, ke), (qo, ko))):
            qh = q_ref[:, qoff:qoff + LANES]
            scores, values = [], []
            for si, (k_ref, v_ref) in enumerate(kv_refs):
                if band and si == 0:
                    kh = k_ref[pl.ds(start, win), koff:koff + LANES]
                    vh = v_ref[pl.ds(start, win), koff:koff + LANES]
                    s = _dot_nt(qh, kh) + bias_ref[...]
                else:
                    kh = k_ref[:, koff:koff + LANES]
                    vh = v_ref[:, koff:koff + LANES]
                    s = _dot_nt(qh, kh)
                scores.append(s)
                values.append(vh)
            m = functools.reduce(jnp.maximum, [jnp.max(s, axis=-1, keepdims=True) for s in scores])
            if use_sink:
                sk = sink_ref[pl.program_id(1) * heads_per_step + 2 * pi + parity] * LOG2_E
                m = jnp.maximum(m, sk)
            es = [jnp.exp2(s - m) for s in scores]
            pv = functools.reduce(jnp.add, [_dot(e.astype(BF16), vh) for e, vh in zip(es, values)])
            if mxu_denom:
                denom = pltpu.roll(pv, HEAD_DIM, 1)
            else:
                denom = functools.reduce(jnp.add, [jnp.sum(e, axis=-1, keepdims=True) for e in es])
            if use_sink:
                denom = denom + jnp.exp2(sk - m)
            halves.append(pv / denom)
        o_ref[:, oo:oo + LANES] = jnp.where(low_half, halves[0], halves[1]).astype(o_ref.dtype)


def _attention(q, segs, *, gqa, steps, tq, band=False, sink=None):
    b, sq, wq_total = q.shape
    wk_total = segs[0][0].shape[2]
    wo_total = N_HEADS * HEAD_DIM
    wq, wk, wo = wq_total // steps, wk_total // steps, wo_total // steps
    n_pairs = wo // LANES
    if gqa:
        pairs = [(LANES * p, LANES * p, 2 * LANES * (p // 2), 2 * LANES * (p // 2) + LANES, LANES * p)
                 for p in range(n_pairs)]
    else:
        pairs = [(2 * LANES * p, 2 * LANES * p + LANES, 2 * LANES * p, 2 * LANES * p + LANES, LANES * p)
                 for p in range(n_pairs)]
    ins, specs = [], []
    if sink is not None:
        ins.append(sink)
        specs.append(pl.BlockSpec(memory_space=pltpu.SMEM))
    if band:
        win, n_q = tq + 2 * WINDOW, sq // tq
        rel = np.arange(win)[None, :] - np.arange(tq)[:, None]
        bias = np.stack([np.where(np.abs(rel - off) <= WINDOW, 0.0, NEG_BIG) for off in (0, WINDOW, 2 * WINDOW)])
        ins.append(jnp.asarray(bias, F32))
        specs.append(pl.BlockSpec(
            (None, tq, win), lambda bi, si, qi: (jnp.where(qi == 0, 0, jnp.where(qi == n_q - 1, 2, 1)), 0, 0)))
    ins.append(q)
    specs.append(pl.BlockSpec((None, tq, wq), lambda bi, si, qi: (bi, qi, si)))
    for k, v in segs:
        sk = k.shape[1]
        ins += [k, v]
        specs += [pl.BlockSpec((None, sk, wk), lambda bi, si, qi: (bi, 0, si))] * 2
    return pl.pallas_call(
        functools.partial(_attn_kernel, pairs=pairs, n_seg=len(segs), tq=tq, band=band,
                          use_sink=sink is not None, heads_per_step=2 * n_pairs,
                          mxu_denom=sum(k.shape[1] for k, _ in segs) >= MXU_DENOM_MIN_KEYS),
        grid=(b, steps, sq // tq),
        in_specs=specs,
        out_specs=pl.BlockSpec((None, tq, wo), lambda bi, si, qi: (bi, qi, si)),
        out_shape=jax.ShapeDtypeStruct((b, sq, wo_total), BF16),
        compiler_params=_cparams(("parallel", "parallel", "parallel")),
        name="attention",
    )(*ins)


def _route(logits_t, bias_col):
    s = _sigmoid(logits_t)
    sb = s + bias_col
    r = [sb[e:e + 1, :] for e in range(N_EXPERTS)]
    sr = [s[e:e + 1, :] for e in range(N_EXPERTS)]
    gscore = []
    for g in range(N_GROUPS):
        a = r[EXPERTS_PER_GROUP * g:EXPERTS_PER_GROUP * (g + 1)]
        best = None
        for i in range(EXPERTS_PER_GROUP):
            for j in range(i + 1, EXPERTS_PER_GROUP):
                p = a[i] + a[j]
                best = p if best is None else jnp.maximum(best, p)
        gscore.append(best)
    gbest, gsel = gscore[0], jnp.zeros_like(gscore[0], dtype=jnp.int32)
    for g in range(1, N_GROUPS):
        better = gscore[g] > gbest
        gsel = jnp.where(better, g, gsel)
        gbest = jnp.where(better, gscore[g], gbest)
    neg = jnp.full_like(r[0], -jnp.inf)
    cand = [jnp.where(gsel == e // EXPERTS_PER_GROUP, r[e], neg) for e in range(N_EXPERTS)]

    def first_argmax(vals):
        best, idx, w = vals[0], jnp.zeros_like(gsel), sr[0]
        for e in range(1, N_EXPERTS):
            better = vals[e] > best
            idx = jnp.where(better, e, idx)
            w = jnp.where(better, sr[e], w)
            best = jnp.where(better, vals[e], best)
        return idx, w

    i1, w1 = first_argmax(cand)
    i2, w2 = first_argmax([jnp.where(i1 == e, neg, cand[e]) for e in range(N_EXPERTS)])
    tot = w1 + w2
    return i1, i2, w1 / tot, w2 / tot


def _moe_pre_kernel(x_ref, o_ref, mod_ref, g_ref, wo_ref, wr_hi_ref, wr_lo_ref, rb_ref, tri_ref,
                    xnew_ref, slot_ref, wts_ref, meta_ref):
    mod = mod_ref[0]
    xn = x_ref[...] + mod[2:3] * _dot(o_ref[...], wo_ref[...])
    xnew_ref[...] = xn
    h = _modulate(xn, g_ref[...], mod[3:4], mod[4:5])
    h_hi = h.astype(BF16)
    h_lo = (h - h_hi.astype(F32)).astype(BF16)
    logits_t = _dot_nt(wr_hi_ref[...], h_hi) + _dot_nt(wr_hi_ref[...], h_lo) + _dot_nt(wr_lo_ref[...], h_hi)
    i1, i2, g1, g2 = _route(logits_t, rb_ref[...])
    pad = jnp.zeros((LANES - 2, g1.shape[1]), F32)
    wts_ref[...] = jnp.concatenate([g1, g2, pad], axis=0).T

    rows = lax.broadcasted_iota(jnp.int32, logits_t.shape, 0)
    member = jnp.where((rows == i1) | (rows == i2), 1.0, 0.0)
    rank = _dot(member.astype(BF16), tri_ref[...])
    count = jnp.sum(member, axis=1, keepdims=True)
    lane = lax.broadcasted_iota(jnp.int32, (1, LANES), 1)
    start = jnp.zeros((1, 1), F32)
    slot1 = jnp.zeros_like(g1)
    slot2 = jnp.zeros_like(g1)
    starts = jnp.zeros((1, LANES), F32)
    chunks = jnp.zeros((1, LANES), F32)
    for e in range(N_EXPERTS):
        pos = start + rank[e:e + 1, :]
        slot1 = jnp.where(i1 == e, pos, slot1)
        slot2 = jnp.where(i2 == e, pos, slot2)
        cnt = count[e:e + 1, :]
        starts = jnp.where(lane == e, start, starts)
        chunks = jnp.where(lane == e, jnp.floor((cnt + (MOE_ROWS - 1)) * (1.0 / MOE_ROWS)), chunks)
        start = start + cnt
    sub = D_MODEL // LANES
    slot_ref[...] = slot1.astype(jnp.int32) * sub + slot2.astype(jnp.int32) * (sub << SLOT_BITS)
    meta_ref[...] = jnp.concatenate([starts, chunks], axis=0).astype(jnp.int32)


def _moe_pre(x, o, mods, gain, wo, wr_hi, wr_lo, rbias, *, latent):
    n = x.shape[0]
    tm = MOE_BLOCK
    const = lambda i: (0, 0)
    tri = jnp.asarray(np.triu(np.ones((tm, tm), np.float32), k=1), BF16)
    return pl.pallas_call(
        _moe_pre_kernel,
        grid=(n // tm,),
        in_specs=[
            pl.BlockSpec((tm, D_MODEL), lambda i: (i, 0)),
            pl.BlockSpec((tm, D_MODEL), lambda i: (i, 0)),
            _mod_spec((2048 // tm) if latent else 0),
            pl.BlockSpec((1, D_MODEL), const),
            pl.BlockSpec((D_MODEL, D_MODEL), const),
            pl.BlockSpec((LANES, D_MODEL), const),
            pl.BlockSpec((LANES, D_MODEL), const),
            pl.BlockSpec((LANES, 1), const),
            pl.BlockSpec((tm, tm), const),
        ],
        out_specs=[
            pl.BlockSpec((tm, D_MODEL), lambda i: (i, 0)),
            pl.BlockSpec((None, 1, tm), lambda i: (i, 0, 0)),
            pl.BlockSpec((tm, LANES), lambda i: (i, 0)),
            pl.BlockSpec((None, 2, LANES), lambda i: (i, 0, 0)),
        ],
        out_shape=[
            jax.ShapeDtypeStruct((n, D_MODEL), F32),
            jax.ShapeDtypeStruct((n // tm, 1, tm), jnp.int32),
            jax.ShapeDtypeStruct((n, LANES), F32),
            jax.ShapeDtypeStruct((n // tm, 2, LANES), jnp.int32),
        ],
        compiler_params=_cparams(("parallel",)),
        name="moe_pre_lat" if latent else "moe_pre_ctx",
    )(x, o, mods, gain.reshape(1, D_MODEL), wo, wr_hi, wr_lo, rbias, tri)


def _moe_kernel(slot_ref, start_ref, chunks_ref,
                xnew_ref, wts_ref, mod_ref, g_ref, wgu_ref, wd_ref, fin_ref, out_ref,
                tok_ref, tok2_ref, xs_ref, y_ref, *, tb, final):
    mask = (1 << SLOT_BITS) - 1
    b, e = pl.program_id(0), pl.program_id(1)
    tok0 = b * tb
    sub = D_MODEL // LANES

    @pl.when(e == 0)
    def _():
        @pl.when(b == 0)
        def _():
            xs_ref[2 * tb * sub:, :] = jnp.zeros((MOE_ROWS * sub, LANES), F32)

        mod = mod_ref[0]
        h = _modulate(xnew_ref[...], g_ref[...], mod[3:4], mod[4:5])
        for s in range(sub):
            tok_ref[pl.ds(s, tb, stride=sub), :] = h[:, LANES * s:LANES * (s + 1)]

        def dispatch(i, carry):
            for k in range(MOE_UNROLL):
                t = i * MOE_UNROLL + k
                row = tok_ref[pl.ds(pl.multiple_of(t * sub, sub), sub), :]
                packed = slot_ref[tok0 + t]
                xs_ref[pl.ds(pl.multiple_of(packed & mask, sub), sub), :] = row
                xs_ref[pl.ds(pl.multiple_of(packed >> SLOT_BITS, sub), sub), :] = row
            return carry
        lax.fori_loop(0, tb // MOE_UNROLL, dispatch, 0)

    def chunk(k, c):
        meta = b * N_EXPERTS + e * MOE_STEP_EXPERTS + k
        base = pl.multiple_of((start_ref[meta] + c * MOE_ROWS) * sub, sub)
        rows_in = xs_ref.at[pl.ds(base, MOE_ROWS * sub)]
        lhs = jnp.concatenate([rows_in[pl.ds(s, MOE_ROWS, stride=sub), :] for s in range(sub)], axis=1)
        au = _dot(lhs.astype(BF16), wgu_ref[k])
        a, u = au[:, :D_EXPERT], au[:, D_EXPERT:]
        y = _dot((a * _sigmoid(a) * u).astype(BF16), wd_ref[k])
        rows_out = y_ref.at[pl.ds(base, MOE_ROWS * sub)]
        for s in range(sub):
            rows_out[pl.ds(s, MOE_ROWS, stride=sub), :] = y[:, LANES * s:LANES * (s + 1)]

    n_chunks = [chunks_ref[b * N_EXPERTS + e * MOE_STEP_EXPERTS + k] for k in range(MOE_STEP_EXPERTS)]
    single = functools.reduce(jnp.logical_and, [n <= 1 for n in n_chunks])

    @pl.when(single)
    def _():
        for k in range(MOE_STEP_EXPERTS):
            chunk(k, 0)

    @pl.when(jnp.logical_not(single))
    def _():
        for k in range(MOE_STEP_EXPERTS):
            def body(c, carry, k=k):
                chunk(k, c)
                return carry
            lax.fori_loop(0, n_chunks[k], body, 0)

    @pl.when(e == N_EXPERTS // MOE_STEP_EXPERTS - 1)
    def _():
        def combine(i, carry):
            for k in range(MOE_UNROLL):
                t = i * MOE_UNROLL + k
                packed = slot_ref[tok0 + t]
                dst = pl.ds(pl.multiple_of(t * sub, sub), sub)
                tok_ref[dst, :] = y_ref[pl.ds(pl.multiple_of(packed & mask, sub), sub), :]
                tok2_ref[dst, :] = y_ref[pl.ds(pl.multiple_of(packed >> SLOT_BITS, sub), sub), :]
            return carry
        lax.fori_loop(0, tb // MOE_UNROLL, combine, 0)
        first = jnp.concatenate([tok_ref[pl.ds(s, tb, stride=sub), :] for s in range(sub)], axis=1)
        second = jnp.concatenate([tok2_ref[pl.ds(s, tb, stride=sub), :] for s in range(sub)], axis=1)
        wts = wts_ref[...]
        moe = wts[:, 0:1] * first + wts[:, 1:2] * second
        out = xnew_ref[...] + mod_ref[0][5:6] * moe
        if final:
            out = _rms(out, fin_ref[...])
        out_ref[...] = out


def _moe(xnew, slots, wts, meta, mods, gain, wgu, wd, fin, *, latent, final):
    n = xnew.shape[0]
    tb = MOE_BLOCK
    cap = 2 * tb + MOE_ROWS
    sub = D_MODEL // LANES
    assert cap * sub < (1 << SLOT_BITS)
    const = lambda i, e: (0, 0)
    rows_per_cond = (2048 // tb) if latent else 0
    if rows_per_cond == 0:
        mod_spec = pl.BlockSpec((1, 6, D_MODEL), lambda i, e: (0, 0, 0))
    else:
        mod_spec = pl.BlockSpec((1, 6, D_MODEL), lambda i, e: (1 + i // rows_per_cond, 0, 0))
    smem = pl.BlockSpec(memory_space=pltpu.SMEM)
    return pl.pallas_call(
        functools.partial(_moe_kernel, tb=tb, final=final),
        grid=(n // tb, N_EXPERTS // MOE_STEP_EXPERTS),
        in_specs=[smem] * 3 + [
            pl.BlockSpec((tb, D_MODEL), lambda i, e: (i, 0)),
            pl.BlockSpec((tb, LANES), lambda i, e: (i, 0)),
            mod_spec,
            pl.BlockSpec((1, D_MODEL), const),
            pl.BlockSpec((MOE_STEP_EXPERTS, D_MODEL, 2 * D_EXPERT), lambda i, e: (e, 0, 0)),
            pl.BlockSpec((MOE_STEP_EXPERTS, D_EXPERT, D_MODEL), lambda i, e: (e, 0, 0)),
            pl.BlockSpec((1, D_MODEL), const),
        ],
        out_specs=pl.BlockSpec((tb, D_MODEL), lambda i, e: (i, 0)),
        out_shape=jax.ShapeDtypeStruct((n, D_MODEL), F32),
        scratch_shapes=[
            pltpu.VMEM((tb * sub, LANES), F32),
            pltpu.VMEM((tb * sub, LANES), F32),
            pltpu.VMEM((cap * sub, LANES), F32),
            pltpu.VMEM((cap * sub, LANES), F32),
        ],
        compiler_params=_cparams(("arbitrary", "arbitrary")),
        name="moe_lat" if latent else "moe_ctx",
    )(slots.reshape(n), meta[:, 0, :N_EXPERTS].reshape(-1), meta[:, 1, :N_EXPERTS].reshape(-1),
      xnew, wts, mods, gain.reshape(1, D_MODEL), wgu, wd, fin.reshape(1, D_MODEL))


def _expert_weights_kernel(wg_ref, wu_ref, wd_ref, wgu_ref, wdn_ref):
    wgu_ref[:, :D_EXPERT] = wg_ref[...].astype(BF16)
    wgu_ref[:, D_EXPERT:] = wu_ref[...].astype(BF16)
    wdn_ref[...] = wd_ref[...].astype(BF16)


def _expert_weights(w_gate, w_up, w_down):
    up = pl.BlockSpec((None, None, D_MODEL, D_EXPERT), lambda l, e: (l, e, 0, 0))
    down = pl.BlockSpec((None, None, D_EXPERT, D_MODEL), lambda l, e: (l, e, 0, 0))
    return pl.pallas_call(
        _expert_weights_kernel,
        grid=(DEPTH, N_EXPERTS),
        in_specs=[up, up, down],
        out_specs=[pl.BlockSpec((None, None, D_MODEL, 2 * D_EXPERT), lambda l, e: (l, e, 0, 0)), down],
        out_shape=[jax.ShapeDtypeStruct((DEPTH, N_EXPERTS, D_MODEL, 2 * D_EXPERT), BF16),
                   jax.ShapeDtypeStruct((DEPTH, N_EXPERTS, D_EXPERT, D_MODEL), BF16)],
        compiler_params=_cparams(("parallel", "parallel")),
        name="expert_weights",
    )(w_gate, w_up, w_down)


def _cache_head_layout(c, fill):
    z = jnp.full_like(c, fill)
    even = jnp.concatenate([c, z], axis=-1)
    odd = jnp.concatenate([z, c], axis=-1)
    return jnp.stack([even, odd], axis=3).reshape(c.shape[0], c.shape[1], 2 * N_KV_HEADS * LANES).astype(BF16)


def _mla_weights(wdq, wdkv, wuq, wukv):
    wd = jnp.concatenate([wdq, wdkv, jnp.zeros((D_MODEL, MLA_DOWN - MLA_Q_LORA - MLA_KV_LORA - MLA_ROPE), F32)], axis=1)
    dk = MLA_NOPE + MLA_ROPE
    wuq_p = jnp.pad(wuq.reshape(MLA_Q_LORA, MLA_HEADS, dk), ((0, 0), (0, 0), (0, LANES - dk)))
    wuq_p = wuq_p.reshape(MLA_Q_LORA, MLA_HEADS * LANES)
    kv = wukv.reshape(MLA_KV_LORA, MLA_HEADS, MLA_NOPE + MLA_V)
    k_nope, v = kv[..., :MLA_NOPE], kv[..., MLA_NOPE:]
    wk_top = jnp.pad(k_nope, ((0, 0), (0, 0), (0, LANES - MLA_NOPE))).reshape(MLA_KV_LORA, MLA_HEADS * LANES)
    place = np.zeros((MLA_CK - MLA_KV_LORA, MLA_HEADS, LANES), np.float32)
    for r in range(MLA_ROPE):
        place[r, :, MLA_NOPE + r] = 1.0
    wk = jnp.concatenate([wk_top, jnp.asarray(place.reshape(MLA_CK - MLA_KV_LORA, MLA_HEADS * LANES))], axis=0)
    z = jnp.zeros_like(v)
    v_even = jnp.concatenate([v, z], axis=-1)
    v_odd = jnp.concatenate([z, v], axis=-1)
    parity = (jnp.arange(MLA_HEADS) % 2 == 0)[None, :, None]
    wv_top = jnp.where(parity, v_even, v_odd).reshape(MLA_KV_LORA, MLA_HEADS * LANES)
    sums = np.zeros((MLA_CK - MLA_KV_LORA, MLA_HEADS, LANES), np.float32)
    sums[MLA_ROPE, 0::2, MLA_V:] = 1.0
    sums[MLA_ROPE, 1::2, :MLA_V] = 1.0
    wv = jnp.concatenate([wv_top, jnp.asarray(sums.reshape(MLA_CK - MLA_KV_LORA, MLA_HEADS * LANES))], axis=0)
    return wd.astype(BF16), wuq_p.astype(BF16), wk.astype(BF16), wv.astype(BF16)


def kernel(x_prompt, x_sample, cache_swa_k, cache_swa_v, cache_qkn_k, cache_qkn_v, cache_mla_ckv, cache_mla_kpe, c, c_ctx, w_mod, b_mod, norm_mix, norm_ffn, swa_wqkv, swa_wo, swa_sink, qkn_wqkv, qkn_wo, qkn_qnorm, qkn_knorm, mla_wdq, mla_qnorm, mla_wuq, mla_wdkv, mla_kvnorm, mla_wukv, mla_wo, w_router, router_bias, moe_w_gate, moe_w_up, moe_w_down, final_norm):
    nb, sc, _ = x_prompt.shape
    nd, sl, _ = x_sample.shape
    past = cache_swa_k.shape[2]
    cond = jnp.zeros((COND_ROWS, D_MODEL), F32).at[0].set(c_ctx).at[1:1 + nd].set(c)
    mods = _adaln(cond, w_mod, b_mod)

    gqa_tables = _rope_tables(sl, HEAD_DIM, 0, HEAD_DIM)
    mla_q_tables = _rope_tables(sl, MLA_ROPE, MLA_NOPE, 0)
    mla_k_tables = _rope_tables(sl, MLA_ROPE, 0, 0)

    wr_t = jnp.zeros((LANES, D_MODEL), F32).at[:N_EXPERTS].set(w_router.T)
    wr_hi = wr_t.astype(BF16)
    wr_lo = (wr_t - wr_hi.astype(F32)).astype(BF16)
    rbias = jnp.zeros((LANES, 1), F32).at[:N_EXPERTS, 0].set(router_bias)
    wgu, wdn = _expert_weights(moe_w_gate, moe_w_up, moe_w_down)

    xc = x_prompt.reshape(nb * sc, D_MODEL)
    xl = x_sample.reshape(nd * sl, D_MODEL)
    tm_c, tm_l = PROJ_TOKENS, PROJ_TOKENS
    states = {k: [] for k in ("swa_k", "swa_v", "qkn_k", "qkn_v", "ckv", "kpe")}
    for i in range(DEPTH):
        kind, j = i % N_MIXERS, i // N_MIXERS
        m = mods[i]
        if kind in (0, 1):
            if kind == 0:
                w, wo, qn, kn = swa_wqkv[j], swa_wo[j], None, None
                ck, cv, sink = cache_swa_k[:, j], cache_swa_v[:, j], swa_sink[j]
            else:
                w, wo, qn, kn = qkn_wqkv[j], qkn_wo[j], qkn_qnorm[j], qkn_knorm[j]
                ck, cv, sink = cache_qkn_k[:, j], cache_qkn_v[:, j], None
            w = w.astype(BF16)
            qc, kc, vc, k32, v32 = _proj_gqa(xc, m, norm_mix[i], w, qn, kn, None, latent=False, tm=tm_c)
            ql, kl, vl = _proj_gqa(xl, m, norm_mix[i], w, qn, kn, gqa_tables, latent=True, tm=tm_l)
            wide = N_HEADS * HEAD_DIM
            oc = _attention(qc.reshape(nb, sc, wide), [(kc.reshape(nb, sc, wide), vc.reshape(nb, sc, wide))],
                            gqa=True, steps=1, tq=sc, sink=sink)
            ol = _attention(ql.reshape(nd, sl, wide),
                            [(kl.reshape(nd, sl, wide), vl.reshape(nd, sl, wide)),
                             (_cache_head_layout(ck, 0.0), _cache_head_layout(cv, 1.0))],
                            gqa=True, steps=N_KV_HEADS // 2, tq=ATTN_Q_BAND if kind == 0 else ATTN_Q,
                            band=(kind == 0), sink=sink)
            names = ("swa_k", "swa_v") if kind == 0 else ("qkn_k", "qkn_v")
            states[names[0]].append(k32.reshape(nb, sc, N_KV_HEADS, HEAD_DIM))
            states[names[1]].append(v32.reshape(nb, sc, N_KV_HEADS, HEAD_DIM))
        else:
            wd, wuq, wk, wv = _mla_weights(mla_wdq[j], mla_wdkv[j], mla_wuq[j], mla_wukv[j])
            wo = mla_wo[j]
            qc, ckc, ckv32, kpe32 = _proj_mla(xc, m, norm_mix[i], wd, mla_qnorm[j], wuq, mla_kvnorm[j],
                                              None, None, latent=False, tm=tm_c)
            ql, ckl = _proj_mla(xl, m, norm_mix[i], wd, mla_qnorm[j], wuq, mla_kvnorm[j],
                                mla_q_tables, mla_k_tables, latent=True, tm=tm_l)
            cache = jnp.concatenate(
                [cache_mla_ckv[:, j], cache_mla_kpe[:, j], jnp.ones((nd, past, 1), F32),
                 jnp.zeros((nd, past, MLA_CK - MLA_KV_LORA - MLA_ROPE - 1), F32)], axis=-1).astype(BF16)
            ck_all = jnp.concatenate([ckl.reshape(nd, sl, MLA_CK), cache], axis=1)
            wq = MLA_HEADS * LANES
            kc, vc = _kv_expand(ckc, wk, wv, tm_c)
            kl, vl = _kv_expand(ck_all.reshape(nd * (sl + past), MLA_CK), wk, wv, 512)
            oc = _attention(qc.reshape(nb, sc, wq), [(kc.reshape(nb, sc, wq), vc.reshape(nb, sc, wq))],
                            gqa=False, steps=1, tq=sc)
            ol = _attention(ql.reshape(nd, sl, wq), [(kl.reshape(nd, sl + past, wq), vl.reshape(nd, sl + past, wq))],
                            gqa=False, steps=MLA_HEADS // 4, tq=ATTN_Q_MLA)
            states["ckv"].append(ckv32.reshape(nb, sc, MLA_KV_LORA))
            states["kpe"].append(kpe32.reshape(nb, sc, MLA_ROPE))
        final = i == DEPTH - 1
        pre = (m, norm_ffn[i], wo.astype(BF16), wr_hi, wr_lo, rbias)
        post = (m, norm_ffn[i], wgu[i], wdn[i], final_norm)
        xc, slots, wts, meta = _moe_pre(xc, oc.reshape(nb * sc, D_MODEL), *pre, latent=False)
        xc = _moe(xc, slots, wts, meta, *post, latent=False, final=final)
        xl, slots, wts, meta = _moe_pre(xl, ol.reshape(nd * sl, D_MODEL), *pre, latent=True)
        xl = _moe(xl, slots, wts, meta, *post, latent=True, final=final)
    return (xc.reshape(nb, sc, D_MODEL), xl.reshape(nd, sl, D_MODEL),
            jnp.stack(states["swa_k"], axis=1), jnp.stack(states["swa_v"], axis=1),
            jnp.stack(states["qkn_k"], axis=1), jnp.stack(states["qkn_v"], axis=1),
            jnp.stack(states["ckv"], axis=1), jnp.stack(states["kpe"], axis=1))
```

```python
import functools

import numpy as np
import jax
import jax.numpy as jnp
from jax import lax
from jax.experimental import pallas as pl
from jax.experimental.pallas import tpu as pltpu

F32 = jnp.float32
BF16 = jnp.bfloat16

D_MODEL = 1024
DEPTH = 4
GRID_W = 64
N_MIXERS = 3
N_HEADS = 16
N_KV_HEADS = 4
HEAD_DIM = 64
QKV_DIM = (N_HEADS + 2 * N_KV_HEADS) * HEAD_DIM
WINDOW = 128
ROPE_THETA = 10000.0
MLA_HEADS = 16
MLA_Q_LORA = 384
MLA_KV_LORA = 256
MLA_NOPE = 64
MLA_ROPE = 32
MLA_V = 64
N_EXPERTS = 16
N_GROUPS = 4
EXPERTS_PER_GROUP = N_EXPERTS // N_GROUPS
D_EXPERT = 256
EPS = 1e-6

LANES = 128
COND_ROWS = 16
MLA_DOWN = 768
MLA_CK = 384
PROJ_TOKENS = 512
ATTN_Q = 256
ATTN_Q_MLA = 512
ATTN_Q_BAND = 256
MOE_BLOCK = 1024
MOE_ROWS = 256
MOE_STEP_EXPERTS = 2
SLOT_BITS = 16
MOE_UNROLL = 32
NEG_BIG = -1e30
MXU_DENOM_MIN_KEYS = 1024
LOG2_E = 1.4426950408889634
VMEM_LIMIT = 56 * 1024 * 1024


def _cparams(sem):
    return pltpu.CompilerParams(dimension_semantics=sem, vmem_limit_bytes=VMEM_LIMIT)


def _sigmoid(x):
    return 1.0 / (1.0 + jnp.exp(-x))


def _rms(x, g):
    return x * lax.rsqrt(jnp.mean(x * x, axis=-1, keepdims=True) + EPS) * g


def _modulate(x, g, shift, scale):
    return x * lax.rsqrt(jnp.mean(x * x, axis=-1, keepdims=True) + EPS) * (g * (1.0 + scale)) + shift


def _dot(a, b):
    return jnp.dot(a, b, preferred_element_type=F32)


def _dot_nt(a, b):
    return lax.dot_general(a, b, (((1,), (1,)), ((), ())), preferred_element_type=F32)


def _adaln_kernel(cond_ref, w_ref, b_ref, o_ref):
    c = cond_ref[...]
    s = (c * _sigmoid(c)).astype(BF16)
    o_ref[...] = _dot(s, w_ref[...].astype(BF16)) + b_ref[...]


def _adaln(cond, w_mod, b_mod):
    tn = 1536
    n = 6 * D_MODEL
    out = pl.pallas_call(
        _adaln_kernel,
        grid=(DEPTH, n // tn),
        in_specs=[
            pl.BlockSpec((COND_ROWS, D_MODEL), lambda l, j: (0, 0)),
            pl.BlockSpec((None, D_MODEL, tn), lambda l, j: (l, 0, j)),
            pl.BlockSpec((None, 1, tn), lambda l, j: (l, 0, j)),
        ],
        out_specs=pl.BlockSpec((None, COND_ROWS, tn), lambda l, j: (l, 0, j)),
        out_shape=jax.ShapeDtypeStruct((DEPTH, COND_ROWS, n), F32),
        compiler_params=_cparams(("parallel", "parallel")),
        name="adaln",
    )(cond, w_mod, b_mod.reshape(DEPTH, 1, n))
    return out.reshape(DEPTH, COND_ROWS, 6, D_MODEL)


def _mod_spec(rows_per_cond):
    if rows_per_cond == 0:
        return pl.BlockSpec((1, 6, D_MODEL), lambda i: (0, 0, 0))
    return pl.BlockSpec((1, 6, D_MODEL), lambda i: (1 + i // rows_per_cond, 0, 0))


def _rope_tables(seq, head_dim, lane0, period):
    half, quarter = head_dim // 2, head_dim // 4
    pos = np.arange(seq)
    row, col = (pos // GRID_W).astype(np.float64), (pos % GRID_W).astype(np.float64)
    inv = ROPE_THETA ** (-np.arange(quarter, dtype=np.float64) / quarter)
    ang = np.concatenate([row[:, None] * inv, col[:, None] * inv], axis=-1)
    cos = np.ones((seq, LANES))
    s1 = np.zeros((seq, LANES))
    s2 = np.zeros((seq, LANES))
    starts = [lane0] if period == 0 else list(range(lane0, LANES, period))
    for st in starts:
        cos[:, st:st + half] = np.cos(ang)
        cos[:, st + half:st + head_dim] = np.cos(ang)
        s1[:, st:st + half] = -np.sin(ang)
        s2[:, st + half:st + head_dim] = np.sin(ang)
    return tuple(jnp.asarray(t, F32) for t in (cos, s1, s2))


def _rope_chunk(x, cos, s1, s2, half):
    return x * cos + pltpu.roll(x, LANES - half, 1) * s1 + pltpu.roll(x, half, 1) * s2


def _head_layout(kv, fill):
    lo = lax.broadcasted_iota(jnp.int32, (kv.shape[0], LANES), 1) < HEAD_DIM
    zero = jnp.full((kv.shape[0], LANES), fill, F32)
    out = []
    for j in range(2):
        ch = kv[:, LANES * j:LANES * (j + 1)]
        ro = pltpu.roll(ch, HEAD_DIM, 1)
        out += [jnp.where(lo, ch, zero), jnp.where(lo, zero, ro), jnp.where(lo, ro, zero), jnp.where(lo, zero, ch)]
    return jnp.concatenate(out, axis=1)


def _proj_gqa_kernel(*refs, qknorm, rope, state):
    it = iter(refs)
    x_ref, mod_ref, g_ref, w_ref = next(it), next(it), next(it), next(it)
    if qknorm:
        qn_ref, kn_ref, gm_ref = next(it), next(it), next(it)
    if rope:
        cos_ref, s1_ref, s2_ref = next(it), next(it), next(it)
    q_ref, kp_ref, vp_ref = next(it), next(it), next(it)
    if state:
        k32_ref, v32_ref = next(it), next(it)

    mod = mod_ref[0]
    h = _modulate(x_ref[...], g_ref[...], mod[0:1], mod[1:2])
    qkv = _dot(h.astype(BF16), w_ref[...])
    nq = N_HEADS * HEAD_DIM
    nk = N_KV_HEADS * HEAD_DIM
    chunks = [qkv[:, LANES * j:LANES * (j + 1)] for j in range((nq + nk) // LANES)]
    if qknorm:
        gains = [qn_ref[...]] * (nq // LANES) + [kn_ref[...]] * (nk // LANES)
        gm = gm_ref[...]
        chunks = [c * lax.rsqrt(_dot((c * c).astype(BF16), gm) + EPS) * g for c, g in zip(chunks, gains)]
    k_state = jnp.concatenate(chunks[nq // LANES:], axis=1)
    if rope:
        cos, s1, s2 = cos_ref[...], s1_ref[...], s2_ref[...]
        chunks = [_rope_chunk(c, cos, s1, s2, HEAD_DIM // 2) for c in chunks]
    q = jnp.concatenate(chunks[:nq // LANES], axis=1) * (HEAD_DIM ** -0.5 * LOG2_E)
    k = jnp.concatenate(chunks[nq // LANES:], axis=1)
    v = qkv[:, nq + nk:]
    q_ref[...] = q.astype(BF16)
    kp_ref[...] = _head_layout(k, 0.0).astype(BF16)
    vp_ref[...] = _head_layout(v, 1.0).astype(BF16)
    if state:
        k32_ref[...] = k_state
        v32_ref[...] = v


def _proj_gqa(x, mods, gain, w, qn, kn, tables, *, latent, tm):
    n = x.shape[0]
    qknorm = qn is not None
    tiles_per_seq = 2048 // tm
    ins = [x, mods, gain.reshape(1, D_MODEL), w]
    specs = [
        pl.BlockSpec((tm, D_MODEL), lambda i: (i, 0)),
        _mod_spec(tiles_per_seq if latent else 0),
        pl.BlockSpec((1, D_MODEL), lambda i: (0, 0)),
        pl.BlockSpec((D_MODEL, QKV_DIM), lambda i: (0, 0)),
    ]
    if qknorm:
        gm = np.kron(np.eye(2), np.full((HEAD_DIM, HEAD_DIM), 1.0 / HEAD_DIM))
        ins += [jnp.tile(qn, 2).reshape(1, LANES), jnp.tile(kn, 2).reshape(1, LANES), jnp.asarray(gm, BF16)]
        specs += [pl.BlockSpec((1, LANES), lambda i: (0, 0))] * 2 + [pl.BlockSpec((LANES, LANES), lambda i: (0, 0))]
    if latent:
        ins += list(tables)
        specs += [pl.BlockSpec((tm, LANES), lambda i: (i % tiles_per_seq, 0))] * 3
    wide = N_HEADS * HEAD_DIM
    outs = [jax.ShapeDtypeStruct((n, wide), BF16)] * 3
    ospecs = [pl.BlockSpec((tm, wide), lambda i: (i, 0))] * 3
    if not latent:
        nk = N_KV_HEADS * HEAD_DIM
        outs += [jax.ShapeDtypeStruct((n, nk), F32)] * 2
        ospecs += [pl.BlockSpec((tm, nk), lambda i: (i, 0))] * 2
    return pl.pallas_call(
        functools.partial(_proj_gqa_kernel, qknorm=qknorm, rope=latent, state=not latent),
        grid=(n // tm,),
        in_specs=specs,
        out_specs=ospecs,
        out_shape=outs,
        compiler_params=_cparams(("parallel",)),
        name="proj_gqa_lat" if latent else "proj_gqa_ctx",
    )(*ins)


def _proj_mla_kernel(*refs, rope, state):
    it = iter(refs)
    x_ref, mod_ref, g_ref, wd_ref, qn_ref, wuq_ref, kvn_ref = (next(it) for _ in range(7))
    if rope:
        qt = [next(it) for _ in range(3)]
        kt = [next(it) for _ in range(3)]
    q_ref, ck_ref = next(it), next(it)
    if state:
        ckv_ref, kpe_ref = next(it), next(it)

    mod = mod_ref[0]
    h = _modulate(x_ref[...], g_ref[...], mod[0:1], mod[1:2])
    d = _dot(h.astype(BF16), wd_ref[...])
    cq = _rms(d[:, :MLA_Q_LORA], qn_ref[...])
    q = _dot(cq.astype(BF16), wuq_ref[...])
    ckv = _rms(d[:, MLA_Q_LORA:MLA_Q_LORA + MLA_KV_LORA], kvn_ref[...])
    kpe = d[:, MLA_Q_LORA + MLA_KV_LORA:]
    kpe_state = kpe
    scale = (MLA_NOPE + MLA_ROPE) ** -0.5 * LOG2_E
    qs = [q[:, LANES * j:LANES * (j + 1)] for j in range(MLA_HEADS)]
    if rope:
        cos, s1, s2 = (t[...] for t in qt)
        qs = [_rope_chunk(c, cos, s1, s2, MLA_ROPE // 2) for c in qs]
        kpe = _rope_chunk(kpe, *(t[...] for t in kt), MLA_ROPE // 2)
    q_ref[...] = (jnp.concatenate(qs, axis=1) * scale).astype(BF16)
    ones = lax.broadcasted_iota(jnp.int32, kpe.shape, 1) == MLA_ROPE
    ck_ref[...] = jnp.concatenate([ckv, jnp.where(ones, 1.0, kpe)], axis=1).astype(BF16)
    if state:
        ckv_ref[...] = ckv
        kpe_ref[...] = kpe_state[:, :MLA_ROPE]


def _proj_mla(x, mods, gain, wd, qn, wuq, kvn, qtables, ktables, *, latent, tm):
    n = x.shape[0]
    tiles_per_seq = 2048 // tm
    wq = MLA_HEADS * LANES
    ins = [x, mods, gain.reshape(1, D_MODEL), wd, qn.reshape(1, MLA_Q_LORA), wuq, kvn.reshape(1, MLA_KV_LORA)]
    specs = [
        pl.BlockSpec((tm, D_MODEL), lambda i: (i, 0)),
        _mod_spec(tiles_per_seq if latent else 0),
        pl.BlockSpec((1, D_MODEL), lambda i: (0, 0)),
        pl.BlockSpec((D_MODEL, MLA_DOWN), lambda i: (0, 0)),
        pl.BlockSpec((1, MLA_Q_LORA), lambda i: (0, 0)),
        pl.BlockSpec((MLA_Q_LORA, wq), lambda i: (0, 0)),
        pl.BlockSpec((1, MLA_KV_LORA), lambda i: (0, 0)),
    ]
    if latent:
        ins += list(qtables) + list(ktables)
        specs += [pl.BlockSpec((tm, LANES), lambda i: (i % tiles_per_seq, 0))] * 6
    outs = [jax.ShapeDtypeStruct((n, wq), BF16), jax.ShapeDtypeStruct((n, MLA_CK), BF16)]
    ospecs = [pl.BlockSpec((tm, wq), lambda i: (i, 0)), pl.BlockSpec((tm, MLA_CK), lambda i: (i, 0))]
    if not latent:
        outs += [jax.ShapeDtypeStruct((n, MLA_KV_LORA), F32), jax.ShapeDtypeStruct((n, MLA_ROPE), F32)]
        ospecs += [pl.BlockSpec((tm, MLA_KV_LORA), lambda i: (i, 0)), pl.BlockSpec((tm, MLA_ROPE), lambda i: (i, 0))]
    return pl.pallas_call(
        functools.partial(_proj_mla_kernel, rope=latent, state=not latent),
        grid=(n // tm,),
        in_specs=specs,
        out_specs=ospecs,
        out_shape=outs,
        compiler_params=_cparams(("parallel",)),
        name="proj_mla_lat" if latent else "proj_mla_ctx",
    )(*ins)


def _kv_expand_kernel(ck_ref, wk_ref, wv_ref, k_ref, v_ref):
    ck = ck_ref[...]
    k_ref[...] = _dot(ck, wk_ref[...]).astype(BF16)
    v_ref[...] = _dot(ck, wv_ref[...]).astype(BF16)


def _kv_expand(ck, wk, wv, tm):
    n = ck.shape[0]
    wide = MLA_HEADS * LANES
    return pl.pallas_call(
        _kv_expand_kernel,
        grid=(n // tm,),
        in_specs=[
            pl.BlockSpec((tm, MLA_CK), lambda i: (i, 0)),
            pl.BlockSpec((MLA_CK, wide), lambda i: (0, 0)),
            pl.BlockSpec((MLA_CK, wide), lambda i: (0, 0)),
        ],
        out_specs=[pl.BlockSpec((tm, wide), lambda i: (i, 0))] * 2,
        out_shape=[jax.ShapeDtypeStruct((n, wide), BF16)] * 2,
        compiler_params=_cparams(("parallel",)),
        name="mla_kv_expand",
    )(ck, wk, wv)


def _attn_kernel(*refs, units, n_seg, tq, band, use_sink, heads_per_step, mxu_denom):
    it = iter(refs)
    if use_sink:
        sink_ref = next(it)
    if band:
        bias_ref = next(it)
    q_ref = next(it)
    kv_refs = [(next(it), next(it)) for _ in range(n_seg)]
    o_ref = next(it)

    if band:
        qi = pl.program_id(2)
        seq = kv_refs[0][0].shape[0]
        win = tq + 2 * WINDOW
        start = pl.multiple_of(jnp.clip(qi * tq - WINDOW, 0, seq - win), WINDOW)

    for members, ke, ko in units:
        rows = tq * len(members)
        low_half = lax.broadcasted_iota(jnp.int32, (rows, LANES), 1) < HEAD_DIM
        halves = []
        for parity, koff in enumerate((ke, ko)):
            qh = jnp.concatenate([q_ref[:, mem[parity]:mem[parity] + LANES] for mem in members], axis=0)
            scores, values = [], []
            for si, (k_ref, v_ref) in enumerate(kv_refs):
                if band and si == 0:
                    kh = k_ref[pl.ds(start, win), koff:koff + LANES]
                    vh = v_ref[pl.ds(start, win), koff:koff + LANES]
                    s = _dot_nt(qh, kh) + jnp.concatenate([bias_ref[...]] * len(members), axis=0)
                else:
                    kh = k_ref[:, koff:koff + LANES]
                    vh = v_ref[:, koff:koff + LANES]
                    s = _dot_nt(qh, kh)
                scores.append(s)
                values.append(vh)
            m = functools.reduce(jnp.maximum, [jnp.max(s, axis=-1, keepdims=True) for s in scores])
            if use_sink:
                head0 = pl.program_id(1) * heads_per_step + parity
                sk = sink_ref[head0 + members[-1][3]] * LOG2_E
                row = lax.broadcasted_iota(jnp.int32, (rows, 1), 0)
                for i, mem in reversed(list(enumerate(members[:-1]))):
                    sk = jnp.where(row < tq * (i + 1), sink_ref[head0 + mem[3]] * LOG2_E, sk)
                m = jnp.maximum(m, sk)
            es = [jnp.exp2(s - m) for s in scores]
            pv = functools.reduce(jnp.add, [_dot(e.astype(BF16), vh) for e, vh in zip(es, values)])
            if mxu_denom:
                denom = pltpu.roll(pv, HEAD_DIM, 1)
            else:
                denom = functools.reduce(jnp.add, [jnp.sum(e, axis=-1, keepdims=True) for e in es])
            if use_sink:
                denom = denom + jnp.exp2(sk - m)
            halves.append(pv / denom)
        out = jnp.where(low_half, halves[0], halves[1]).astype(o_ref.dtype)
        for i, mem in enumerate(members):
            o_ref[:, mem[2]:mem[2] + LANES] = out[tq * i:tq * (i + 1)]


def _attention(q, segs, *, gqa, steps, tq, band=False, sink=None):
    b, sq, wq_total = q.shape
    wk_total = segs[0][0].shape[2]
    wo_total = N_HEADS * HEAD_DIM
    wq, wk, wo = wq_total // steps, wk_total // steps, wo_total // steps
    n_pairs = wo // LANES
    if gqa:
        pair = lambda p: (LANES * p, LANES * p, LANES * p, 2 * p)
        units = [([pair(2 * g), pair(2 * g + 1)], 2 * LANES * g, 2 * LANES * g + LANES) for g in range(n_pairs // 2)]
    else:
        pair = lambda p: (2 * LANES * p, 2 * LANES * p + LANES, LANES * p, 2 * p)
        units = [([pair(p)], 2 * LANES * p, 2 * LANES * p + LANES) for p in range(n_pairs)]
    ins, specs = [], []
    if sink is not None:
        ins.append(sink)
        specs.append(pl.BlockSpec(memory_space=pltpu.SMEM))
    if band:
        win, n_q = tq + 2 * WINDOW, sq // tq
        rel = np.arange(win)[None, :] - np.arange(tq)[:, None]
        bias = np.stack([np.where(np.abs(rel - off) <= WINDOW, 0.0, NEG_BIG) for off in (0, WINDOW, 2 * WINDOW)])
        ins.append(jnp.asarray(bias, F32))
        specs.append(pl.BlockSpec(
            (None, tq, win), lambda bi, si, qi: (jnp.where(qi == 0, 0, jnp.where(qi == n_q - 1, 2, 1)), 0, 0)))
    ins.append(q)
    specs.append(pl.BlockSpec((None, tq, wq), lambda bi, si, qi: (bi, qi, si)))
    for k, v in segs:
        sk = k.shape[1]
        ins += [k, v]
        specs += [pl.BlockSpec((None, sk, wk), lambda bi, si, qi: (bi, 0, si))] * 2
    return pl.pallas_call(
        functools.partial(_attn_kernel, units=units, n_seg=len(segs), tq=tq, band=band,
                          use_sink=sink is not None, heads_per_step=2 * n_pairs,
                          mxu_denom=sum(k.shape[1] for k, _ in segs) >= MXU_DENOM_MIN_KEYS),
        grid=(b, steps, sq // tq),
        in_specs=specs,
        out_specs=pl.BlockSpec((None, tq, wo), lambda bi, si, qi: (bi, qi, si)),
        out_shape=jax.ShapeDtypeStruct((b, sq, wo_total), BF16),
        compiler_params=_cparams(("parallel", "parallel", "parallel")),
        name="attention",
    )(*ins)


def _route(logits_t, bias_col):
    s = _sigmoid(logits_t)
    sb = s + bias_col
    r = [sb[e:e + 1, :] for e in range(N_EXPERTS)]
    sr = [s[e:e + 1, :] for e in range(N_EXPERTS)]
    gscore = []
    for g in range(N_GROUPS):
        a = r[EXPERTS_PER_GROUP * g:EXPERTS_PER_GROUP * (g + 1)]
        best = None
        for i in range(EXPERTS_PER_GROUP):
            for j in range(i + 1, EXPERTS_PER_GROUP):
                p = a[i] + a[j]
                best = p if best is None else jnp.maximum(best, p)
        gscore.append(best)
    gbest, gsel = gscore[0], jnp.zeros_like(gscore[0], dtype=jnp.int32)
    for g in range(1, N_GROUPS):
        better = gscore[g] > gbest
        gsel = jnp.where(better, g, gsel)
        gbest = jnp.where(better, gscore[g], gbest)
    neg = jnp.full_like(r[0], -jnp.inf)
    cand = [jnp.where(gsel == e // EXPERTS_PER_GROUP, r[e], neg) for e in range(N_EXPERTS)]

    def first_argmax(vals):
        best, idx, w = vals[0], jnp.zeros_like(gsel), sr[0]
        for e in range(1, N_EXPERTS):
            better = vals[e] > best
            idx = jnp.where(better, e, idx)
            w = jnp.where(better, sr[e], w)
            best = jnp.where(better, vals[e], best)
        return idx, w

    i1, w1 = first_argmax(cand)
    i2, w2 = first_argmax([jnp.where(i1 == e, neg, cand[e]) for e in range(N_EXPERTS)])
    tot = w1 + w2
    return i1, i2, w1 / tot, w2 / tot


def _moe_pre_kernel(x_ref, o_ref, mod_ref, g_ref, wo_ref, wr_hi_ref, wr_lo_ref, rb_ref, tri_ref,
                    xnew_ref, slot_ref, wts_ref, meta_ref):
    mod = mod_ref[0]
    xn = x_ref[...] + mod[2:3] * _dot(o_ref[...], wo_ref[...])
    xnew_ref[...] = xn
    h = _modulate(xn, g_ref[...], mod[3:4], mod[4:5])
    h_hi = h.astype(BF16)
    h_lo = (h - h_hi.astype(F32)).astype(BF16)
    logits_t = _dot_nt(wr_hi_ref[...], h_hi) + _dot_nt(wr_hi_ref[...], h_lo) + _dot_nt(wr_lo_ref[...], h_hi)
    i1, i2, g1, g2 = _route(logits_t, rb_ref[...])
    pad = jnp.zeros((LANES - 2, g1.shape[1]), F32)
    wts_ref[...] = jnp.concatenate([g1, g2, pad], axis=0).T

    rows = lax.broadcasted_iota(jnp.int32, logits_t.shape, 0)
    member = jnp.where((rows == i1) | (rows == i2), 1.0, 0.0)
    rank = _dot(member.astype(BF16), tri_ref[...])
    count = jnp.sum(member, axis=1, keepdims=True)
    lane = lax.broadcasted_iota(jnp.int32, (1, LANES), 1)
    start = jnp.zeros((1, 1), F32)
    slot1 = jnp.zeros_like(g1)
    slot2 = jnp.zeros_like(g1)
    starts = jnp.zeros((1, LANES), F32)
    chunks = jnp.zeros((1, LANES), F32)
    for e in range(N_EXPERTS):
        pos = start + rank[e:e + 1, :]
        slot1 = jnp.where(i1 == e, pos, slot1)
        slot2 = jnp.where(i2 == e, pos, slot2)
        cnt = count[e:e + 1, :]
        starts = jnp.where(lane == e, start, starts)
        chunks = jnp.where(lane == e, jnp.floor((cnt + (MOE_ROWS - 1)) * (1.0 / MOE_ROWS)), chunks)
        start = start + cnt
    sub = D_MODEL // LANES
    slot_ref[...] = slot1.astype(jnp.int32) * sub + slot2.astype(jnp.int32) * (sub << SLOT_BITS)
    meta_ref[...] = jnp.concatenate([starts, chunks], axis=0).astype(jnp.int32)


def _moe_pre(x, o, mods, gain, wo, wr_hi, wr_lo, rbias, *, latent):
    n = x.shape[0]
    tm = MOE_BLOCK
    const = lambda i: (0, 0)
    tri = jnp.asarray(np.triu(np.ones((tm, tm), np.float32), k=1), BF16)
    return pl.pallas_call(
        _moe_pre_kernel,
        grid=(n // tm,),
        in_specs=[
            pl.BlockSpec((tm, D_MODEL), lambda i: (i, 0)),
            pl.BlockSpec((tm, D_MODEL), lambda i: (i, 0)),
            _mod_spec((2048 // tm) if latent else 0),
            pl.BlockSpec((1, D_MODEL), const),
            pl.BlockSpec((D_MODEL, D_MODEL), const),
            pl.BlockSpec((LANES, D_MODEL), const),
            pl.BlockSpec((LANES, D_MODEL), const),
            pl.BlockSpec((LANES, 1), const),
            pl.BlockSpec((tm, tm), const),
        ],
        out_specs=[
            pl.BlockSpec((tm, D_MODEL), lambda i: (i, 0)),
            pl.BlockSpec((None, 1, tm), lambda i: (i, 0, 0)),
            pl.BlockSpec((tm, LANES), lambda i: (i, 0)),
            pl.BlockSpec((None, 2, LANES), lambda i: (i, 0, 0)),
        ],
        out_shape=[
            jax.ShapeDtypeStruct((n, D_MODEL), F32),
            jax.ShapeDtypeStruct((n // tm, 1, tm), jnp.int32),
            jax.ShapeDtypeStruct((n, LANES), F32),
            jax.ShapeDtypeStruct((n // tm, 2, LANES), jnp.int32),
        ],
        compiler_params=_cparams(("parallel",)),
        name="moe_pre_lat" if latent else "moe_pre_ctx",
    )(x, o, mods, gain.reshape(1, D_MODEL), wo, wr_hi, wr_lo, rbias, tri)


def _moe_kernel(slot_ref, start_ref, chunks_ref,
                xnew_ref, wts_ref, mod_ref, g_ref, wgu_ref, wd_ref, fin_ref, out_ref,
                tok_ref, tok2_ref, xs_ref, y_ref, *, tb, final):
    mask = (1 << SLOT_BITS) - 1
    b, e = pl.program_id(0), pl.program_id(1)
    tok0 = b * tb
    sub = D_MODEL // LANES

    @pl.when(e == 0)
    def _():
        @pl.when(b == 0)
        def _():
            xs_ref[2 * tb * sub:, :] = jnp.zeros((MOE_ROWS * sub, LANES), F32)

        mod = mod_ref[0]
        h = _modulate(xnew_ref[...], g_ref[...], mod[3:4], mod[4:5])
        for s in range(sub):
            tok_ref[pl.ds(s, tb, stride=sub), :] = h[:, LANES * s:LANES * (s + 1)]

        def dispatch(i, carry):
            for k in range(MOE_UNROLL):
                t = i * MOE_UNROLL + k
                row = tok_ref[pl.ds(pl.multiple_of(t * sub, sub), sub), :]
                packed = slot_ref[tok0 + t]
                xs_ref[pl.ds(pl.multiple_of(packed & mask, sub), sub), :] = row
                xs_ref[pl.ds(pl.multiple_of(packed >> SLOT_BITS, sub), sub), :] = row
            return carry
        lax.fori_loop(0, tb // MOE_UNROLL, dispatch, 0)

    def chunk(k, c):
        meta = b * N_EXPERTS + e * MOE_STEP_EXPERTS + k
        base = pl.multiple_of((start_ref[meta] + c * MOE_ROWS) * sub, sub)
        rows_in = xs_ref.at[pl.ds(base, MOE_ROWS * sub)]
        lhs = jnp.concatenate([rows_in[pl.ds(s, MOE_ROWS, stride=sub), :] for s in range(sub)], axis=1)
        au = _dot(lhs.astype(BF16), wgu_ref[k])
        a, u = au[:, :D_EXPERT], au[:, D_EXPERT:]
        y = _dot((a * _sigmoid(a) * u).astype(BF16), wd_ref[k])
        rows_out = y_ref.at[pl.ds(base, MOE_ROWS * sub)]
        for s in range(sub):
            rows_out[pl.ds(s, MOE_ROWS, stride=sub), :] = y[:, LANES * s:LANES * (s + 1)]

    n_chunks = [chunks_ref[b * N_EXPERTS + e * MOE_STEP_EXPERTS + k] for k in range(MOE_STEP_EXPERTS)]
    single = functools.reduce(jnp.logical_and, [n <= 1 for n in n_chunks])

    @pl.when(single)
    def _():
        for k in range(MOE_STEP_EXPERTS):
            chunk(k, 0)

    @pl.when(jnp.logical_not(single))
    def _():
        for k in range(MOE_STEP_EXPERTS):
            def body(c, carry, k=k):
                chunk(k, c)
                return carry
            lax.fori_loop(0, n_chunks[k], body, 0)

    @pl.when(e == N_EXPERTS // MOE_STEP_EXPERTS - 1)
    def _():
        def combine(i, carry):
            for k in range(MOE_UNROLL):
                t = i * MOE_UNROLL + k
                packed = slot_ref[tok0 + t]
                dst = pl.ds(pl.multiple_of(t * sub, sub), sub)
                tok_ref[dst, :] = y_ref[pl.ds(pl.multiple_of(packed & mask, sub), sub), :]
                tok2_ref[dst, :] = y_ref[pl.ds(pl.multiple_of(packed >> SLOT_BITS, sub), sub), :]
            return carry
        lax.fori_loop(0, tb // MOE_UNROLL, combine, 0)
        first = jnp.concatenate([tok_ref[pl.ds(s, tb, stride=sub), :] for s in range(sub)], axis=1)
        second = jnp.concatenate([tok2_ref[pl.ds(s, tb, stride=sub), :] for s in range(sub)], axis=1)
        wts = wts_ref[...]
        moe = wts[:, 0:1] * first + wts[:, 1:2] * second
        out = xnew_ref[...] + mod_ref[0][5:6] * moe
        if final:
            out = _rms(out, fin_ref[...])
        out_ref[...] = out


def _moe(xnew, slots, wts, meta, mods, gain, wgu, wd, fin, *, layer, latent, final):
    n = xnew.shape[0]
    tb = MOE_BLOCK
    cap = 2 * tb + MOE_ROWS
    sub = D_MODEL // LANES
    assert cap * sub < (1 << SLOT_BITS)
    const = lambda i, e: (0, 0)
    rows_per_cond = (2048 // tb) if latent else 0
    if rows_per_cond == 0:
        mod_spec = pl.BlockSpec((1, 6, D_MODEL), lambda i, e: (0, 0, 0))
    else:
        mod_spec = pl.BlockSpec((1, 6, D_MODEL), lambda i, e: (1 + i // rows_per_cond, 0, 0))
    smem = pl.BlockSpec(memory_space=pltpu.SMEM)
    return pl.pallas_call(
        functools.partial(_moe_kernel, tb=tb, final=final),
        grid=(n // tb, N_EXPERTS // MOE_STEP_EXPERTS),
        in_specs=[smem] * 3 + [
            pl.BlockSpec((tb, D_MODEL), lambda i, e: (i, 0)),
            pl.BlockSpec((tb, LANES), lambda i, e: (i, 0)),
            mod_spec,
            pl.BlockSpec((1, D_MODEL), const),
            pl.BlockSpec((None, MOE_STEP_EXPERTS, D_MODEL, 2 * D_EXPERT), lambda i, e: (layer, e, 0, 0)),
            pl.BlockSpec((None, MOE_STEP_EXPERTS, D_EXPERT, D_MODEL), lambda i, e: (layer, e, 0, 0)),
            pl.BlockSpec((1, D_MODEL), const),
        ],
        out_specs=pl.BlockSpec((tb, D_MODEL), lambda i, e: (i, 0)),
        out_shape=jax.ShapeDtypeStruct((n, D_MODEL), F32),
        scratch_shapes=[
            pltpu.VMEM((tb * sub, LANES), F32),
            pltpu.VMEM((tb * sub, LANES), F32),
            pltpu.VMEM((cap * sub, LANES), F32),
            pltpu.VMEM((cap * sub, LANES), F32),
        ],
        compiler_params=_cparams(("arbitrary", "arbitrary")),
        name="moe_lat" if latent else "moe_ctx",
    )(slots.reshape(n), meta[:, 0, :N_EXPERTS].reshape(-1), meta[:, 1, :N_EXPERTS].reshape(-1),
      xnew, wts, mods, gain.reshape(1, D_MODEL), wgu, wd, fin.reshape(1, D_MODEL))


def _expert_weights_kernel(wg_ref, wu_ref, wd_ref, wgu_ref, wdn_ref):
    wgu_ref[:, :D_EXPERT] = wg_ref[...].astype(BF16)
    wgu_ref[:, D_EXPERT:] = wu_ref[...].astype(BF16)
    wdn_ref[...] = wd_ref[...].astype(BF16)


def _expert_weights(w_gate, w_up, w_down):
    up = pl.BlockSpec((None, None, D_MODEL, D_EXPERT), lambda l, e: (l, e, 0, 0))
    down = pl.BlockSpec((None, None, D_EXPERT, D_MODEL), lambda l, e: (l, e, 0, 0))
    return pl.pallas_call(
        _expert_weights_kernel,
        grid=(DEPTH, N_EXPERTS),
        in_specs=[up, up, down],
        out_specs=[pl.BlockSpec((None, None, D_MODEL, 2 * D_EXPERT), lambda l, e: (l, e, 0, 0)), down],
        out_shape=[jax.ShapeDtypeStruct((DEPTH, N_EXPERTS, D_MODEL, 2 * D_EXPERT), BF16),
                   jax.ShapeDtypeStruct((DEPTH, N_EXPERTS, D_EXPERT, D_MODEL), BF16)],
        compiler_params=_cparams(("parallel", "parallel")),
        name="expert_weights",
    )(w_gate, w_up, w_down)


def _cache_head_layout(c, fill):
    z = jnp.full_like(c, fill)
    even = jnp.concatenate([c, z], axis=-1)
    odd = jnp.concatenate([z, c], axis=-1)
    return jnp.stack([even, odd], axis=3).reshape(c.shape[0], c.shape[1], 2 * N_KV_HEADS * LANES).astype(BF16)


def _mla_weights(wdq, wdkv, wuq, wukv):
    wd = jnp.concatenate([wdq, wdkv, jnp.zeros((D_MODEL, MLA_DOWN - MLA_Q_LORA - MLA_KV_LORA - MLA_ROPE), F32)], axis=1)
    dk = MLA_NOPE + MLA_ROPE
    wuq_p = jnp.pad(wuq.reshape(MLA_Q_LORA, MLA_HEADS, dk), ((0, 0), (0, 0), (0, LANES - dk)))
    wuq_p = wuq_p.reshape(MLA_Q_LORA, MLA_HEADS * LANES)
    kv = wukv.reshape(MLA_KV_LORA, MLA_HEADS, MLA_NOPE + MLA_V)
    k_nope, v = kv[..., :MLA_NOPE], kv[..., MLA_NOPE:]
    wk_top = jnp.pad(k_nope, ((0, 0), (0, 0), (0, LANES - MLA_NOPE))).reshape(MLA_KV_LORA, MLA_HEADS * LANES)
    place = np.zeros((MLA_CK - MLA_KV_LORA, MLA_HEADS, LANES), np.float32)
    for r in range(MLA_ROPE):
        place[r, :, MLA_NOPE + r] = 1.0
    wk = jnp.concatenate([wk_top, jnp.asarray(place.reshape(MLA_CK - MLA_KV_LORA, MLA_HEADS * LANES))], axis=0)
    z = jnp.zeros_like(v)
    v_even = jnp.concatenate([v, z], axis=-1)
    v_odd = jnp.concatenate([z, v], axis=-1)
    parity = (jnp.arange(MLA_HEADS) % 2 == 0)[None, :, None]
    wv_top = jnp.where(parity, v_even, v_odd).reshape(MLA_KV_LORA, MLA_HEADS * LANES)
    sums = np.zeros((MLA_CK - MLA_KV_LORA, MLA_HEADS, LANES), np.float32)
    sums[MLA_ROPE, 0::2, MLA_V:] = 1.0
    sums[MLA_ROPE, 1::2, :MLA_V] = 1.0
    wv = jnp.concatenate([wv_top, jnp.asarray(sums.reshape(MLA_CK - MLA_KV_LORA, MLA_HEADS * LANES))], axis=0)
    return wd.astype(BF16), wuq_p.astype(BF16), wk.astype(BF16), wv.astype(BF16)


def kernel(x_prompt, x_sample, cache_swa_k, cache_swa_v, cache_qkn_k, cache_qkn_v, cache_mla_ckv, cache_mla_kpe, c, c_ctx, w_mod, b_mod, norm_mix, norm_ffn, swa_wqkv, swa_wo, swa_sink, qkn_wqkv, qkn_wo, qkn_qnorm, qkn_knorm, mla_wdq, mla_qnorm, mla_wuq, mla_wdkv, mla_kvnorm, mla_wukv, mla_wo, w_router, router_bias, moe_w_gate, moe_w_up, moe_w_down, final_norm):
    nb, sc, _ = x_prompt.shape
    nd, sl, _ = x_sample.shape
    past = cache_swa_k.shape[2]
    cond = jnp.zeros((COND_ROWS, D_MODEL), F32).at[0].set(c_ctx).at[1:1 + nd].set(c)
    mods = _adaln(cond, w_mod, b_mod)

    gqa_tables = _rope_tables(sl, HEAD_DIM, 0, HEAD_DIM)
    mla_q_tables = _rope_tables(sl, MLA_ROPE, MLA_NOPE, 0)
    mla_k_tables = _rope_tables(sl, MLA_ROPE, 0, 0)

    wr_t = jnp.zeros((LANES, D_MODEL), F32).at[:N_EXPERTS].set(w_router.T)
    wr_hi = wr_t.astype(BF16)
    wr_lo = (wr_t - wr_hi.astype(F32)).astype(BF16)
    rbias = jnp.zeros((LANES, 1), F32).at[:N_EXPERTS, 0].set(router_bias)
    wgu, wdn = _expert_weights(moe_w_gate, moe_w_up, moe_w_down)

    xc = x_prompt.reshape(nb * sc, D_MODEL)
    xl = x_sample.reshape(nd * sl, D_MODEL)
    tm_c, tm_l = PROJ_TOKENS, PROJ_TOKENS
    states = {k: [] for k in ("swa_k", "swa_v", "qkn_k", "qkn_v", "ckv", "kpe")}
    for i in range(DEPTH):
        kind, j = i % N_MIXERS, i // N_MIXERS
        m = mods[i]
        if kind in (0, 1):
            if kind == 0:
                w, wo, qn, kn = swa_wqkv[j], swa_wo[j], None, None
                ck, cv, sink = cache_swa_k[:, j], cache_swa_v[:, j], swa_sink[j]
            else:
                w, wo, qn, kn = qkn_wqkv[j], qkn_wo[j], qkn_qnorm[j], qkn_knorm[j]
                ck, cv, sink = cache_qkn_k[:, j], cache_qkn_v[:, j], None
            w = w.astype(BF16)
            qc, kc, vc, k32, v32 = _proj_gqa(xc, m, norm_mix[i], w, qn, kn, None, latent=False, tm=tm_c)
            ql, kl, vl = _proj_gqa(xl, m, norm_mix[i], w, qn, kn, gqa_tables, latent=True, tm=tm_l)
            wide = N_HEADS * HEAD_DIM
            oc = _attention(qc.reshape(nb, sc, wide), [(kc.reshape(nb, sc, wide), vc.reshape(nb, sc, wide))],
                            gqa=True, steps=1, tq=sc, sink=sink)
            ol = _attention(ql.reshape(nd, sl, wide),
                            [(kl.reshape(nd, sl, wide), vl.reshape(nd, sl, wide)),
                             (_cache_head_layout(ck, 0.0), _cache_head_layout(cv, 1.0))],
                            gqa=True, steps=1, tq=ATTN_Q_BAND if kind == 0 else ATTN_Q,
                            band=(kind == 0), sink=sink)
            names = ("swa_k", "swa_v") if kind == 0 else ("qkn_k", "qkn_v")
            states[names[0]].append(k32.reshape(nb, sc, N_KV_HEADS, HEAD_DIM))
            states[names[1]].append(v32.reshape(nb, sc, N_KV_HEADS, HEAD_DIM))
        else:
            wd, wuq, wk, wv = _mla_weights(mla_wdq[j], mla_wdkv[j], mla_wuq[j], mla_wukv[j])
            wo = mla_wo[j]
            qc, ckc, ckv32, kpe32 = _proj_mla(xc, m, norm_mix[i], wd, mla_qnorm[j], wuq, mla_kvnorm[j],
                                              None, None, latent=False, tm=tm_c)
            ql, ckl = _proj_mla(xl, m, norm_mix[i], wd, mla_qnorm[j], wuq, mla_kvnorm[j],
                                mla_q_tables, mla_k_tables, latent=True, tm=tm_l)
            cache = jnp.concatenate(
                [cache_mla_ckv[:, j], cache_mla_kpe[:, j], jnp.ones((nd, past, 1), F32),
                 jnp.zeros((nd, past, MLA_CK - MLA_KV_LORA - MLA_ROPE - 1), F32)], axis=-1).astype(BF16)
            ck_all = jnp.concatenate([ckl.reshape(nd, sl, MLA_CK), cache], axis=1)
            wq = MLA_HEADS * LANES
            kc, vc = _kv_expand(ckc, wk, wv, tm_c)
            kl, vl = _kv_expand(ck_all.reshape(nd * (sl + past), MLA_CK), wk, wv, 512)
            oc = _attention(qc.reshape(nb, sc, wq), [(kc.reshape(nb, sc, wq), vc.reshape(nb, sc, wq))],
                            gqa=False, steps=1, tq=sc)
            ol = _attention(ql.reshape(nd, sl, wq), [(kl.reshape(nd, sl + past, wq), vl.reshape(nd, sl + past, wq))],
                            gqa=False, steps=MLA_HEADS // 8, tq=ATTN_Q_MLA)
            states["ckv"].append(ckv32.reshape(nb, sc, MLA_KV_LORA))
            states["kpe"].append(kpe32.reshape(nb, sc, MLA_ROPE))
        final = i == DEPTH - 1
        pre = (m, norm_ffn[i], wo.astype(BF16), wr_hi, wr_lo, rbias)
        post = (m, norm_ffn[i], wgu, wdn, final_norm)
        xc, slots, wts, meta = _moe_pre(xc, oc.reshape(nb * sc, D_MODEL), *pre, latent=False)
        xc = _moe(xc, slots, wts, meta, *post, layer=i, latent=False, final=final)
        xl, slots, wts, meta = _moe_pre(xl, ol.reshape(nd * sl, D_MODEL), *pre, latent=True)
        xl = _moe(xl, slots, wts, meta, *post, layer=i, latent=True, final=final)
    return (xc.reshape(nb, sc, D_MODEL), xl.reshape(nd, sl, D_MODEL),
            jnp.stack(states["swa_k"], axis=1), jnp.stack(states["swa_v"], axis=1),
            jnp.stack(states["qkn_k"], axis=1), jnp.stack(states["qkn_v"], axis=1),
            jnp.stack(states["ckv"], axis=1), jnp.stack(states["kpe"], axis=1))
```

```python
import functools

import numpy as np
import jax
import jax.numpy as jnp
from jax import lax
from jax.experimental import pallas as pl
from jax.experimental.pallas import tpu as pltpu

F32 = jnp.float32
BF16 = jnp.bfloat16

D_MODEL = 1024
DEPTH = 4
GRID_W = 64
N_MIXERS = 3
N_HEADS = 16
N_KV_HEADS = 4
HEAD_DIM = 64
QKV_DIM = (N_HEADS + 2 * N_KV_HEADS) * HEAD_DIM
WINDOW = 128
ROPE_THETA = 10000.0
MLA_HEADS = 16
MLA_Q_LORA = 384
MLA_KV_LORA = 256
MLA_NOPE = 64
MLA_ROPE = 32
MLA_V = 64
N_EXPERTS = 16
N_GROUPS = 4
EXPERTS_PER_GROUP = N_EXPERTS // N_GROUPS
D_EXPERT = 256
EPS = 1e-6

LANES = 128
COND_ROWS = 16
MLA_DOWN = 768
MLA_CK = 384
PROJ_TOKENS = 512
ATTN_Q = 256
ATTN_Q_MLA = 512
ATTN_Q_BAND = 256
MOE_BLOCK = 1024
MOE_ROWS = 192
MOE_STEP_EXPERTS = 2
SLOT_BITS = 16
MOE_UNROLL = 32
NEG_BIG = -1e30
MXU_DENOM_MIN_KEYS = 1024
LOG2_E = 1.4426950408889634
VMEM_LIMIT = 56 * 1024 * 1024


def _cparams(sem):
    return pltpu.CompilerParams(dimension_semantics=sem, vmem_limit_bytes=VMEM_LIMIT)


def _sigmoid(x):
    return 1.0 / (1.0 + jnp.exp(-x))


def _rms(x, g):
    return x * lax.rsqrt(jnp.mean(x * x, axis=-1, keepdims=True) + EPS) * g


def _modulate(x, g, shift, scale):
    return x * lax.rsqrt(jnp.mean(x * x, axis=-1, keepdims=True) + EPS) * (g * (1.0 + scale)) + shift


def _dot(a, b):
    return jnp.dot(a, b, preferred_element_type=F32)


def _dot_nt(a, b):
    return lax.dot_general(a, b, (((1,), (1,)), ((), ())), preferred_element_type=F32)


def _adaln_kernel(cond_ref, w_ref, b_ref, o_ref):
    c = cond_ref[...]
    s = (c * _sigmoid(c)).astype(BF16)
    o_ref[...] = _dot(s, w_ref[...].astype(BF16)) + b_ref[...]


def _adaln(cond, w_mod, b_mod):
    tn = 1536
    n = 6 * D_MODEL
    out = pl.pallas_call(
        _adaln_kernel,
        grid=(DEPTH, n // tn),
        in_specs=[
            pl.BlockSpec((COND_ROWS, D_MODEL), lambda l, j: (0, 0)),
            pl.BlockSpec((None, D_MODEL, tn), lambda l, j: (l, 0, j)),
            pl.BlockSpec((None, 1, tn), lambda l, j: (l, 0, j)),
        ],
        out_specs=pl.BlockSpec((None, COND_ROWS, tn), lambda l, j: (l, 0, j)),
        out_shape=jax.ShapeDtypeStruct((DEPTH, COND_ROWS, n), F32),
        compiler_params=_cparams(("parallel", "parallel")),
        name="adaln",
    )(cond, w_mod, b_mod.reshape(DEPTH, 1, n))
    return out.reshape(DEPTH, COND_ROWS, 6, D_MODEL)


def _mod_spec(rows_per_cond):
    if rows_per_cond == 0:
        return pl.BlockSpec((1, 6, D_MODEL), lambda i: (0, 0, 0))
    return pl.BlockSpec((1, 6, D_MODEL), lambda i: (1 + i // rows_per_cond, 0, 0))


def _rope_tables(seq, head_dim, lane0, period):
    half, quarter = head_dim // 2, head_dim // 4
    pos = np.arange(seq)
    row, col = (pos // GRID_W).astype(np.float64), (pos % GRID_W).astype(np.float64)
    inv = ROPE_THETA ** (-np.arange(quarter, dtype=np.float64) / quarter)
    ang = np.concatenate([row[:, None] * inv, col[:, None] * inv], axis=-1)
    cos = np.ones((seq, LANES))
    s1 = np.zeros((seq, LANES))
    s2 = np.zeros((seq, LANES))
    starts = [lane0] if period == 0 else list(range(lane0, LANES, period))
    for st in starts:
        cos[:, st:st + half] = np.cos(ang)
        cos[:, st + half:st + head_dim] = np.cos(ang)
        s1[:, st:st + half] = -np.sin(ang)
        s2[:, st + half:st + head_dim] = np.sin(ang)
    return tuple(jnp.asarray(t, F32) for t in (cos, s1, s2))


def _rope_chunk(x, cos, s1, s2, half):
    return x * cos + pltpu.roll(x, LANES - half, 1) * s1 + pltpu.roll(x, half, 1) * s2


def _head_layout(kv, fill):
    lo = lax.broadcasted_iota(jnp.int32, (kv.shape[0], LANES), 1) < HEAD_DIM
    zero = jnp.full((kv.shape[0], LANES), fill, F32)
    out = []
    for j in range(2):
        ch = kv[:, LANES * j:LANES * (j + 1)]
        ro = pltpu.roll(ch, HEAD_DIM, 1)
        out += [jnp.where(lo, ch, zero), jnp.where(lo, zero, ro), jnp.where(lo, ro, zero), jnp.where(lo, zero, ch)]
    return jnp.concatenate(out, axis=1)


def _proj_gqa_kernel(*refs, qknorm, rope, state):
    it = iter(refs)
    x_ref, mod_ref, g_ref, w_ref = next(it), next(it), next(it), next(it)
    if qknorm:
        qn_ref, kn_ref, gm_ref = next(it), next(it), next(it)
    if rope:
        cos_ref, s1_ref, s2_ref = next(it), next(it), next(it)
    q_ref, kp_ref, vp_ref = next(it), next(it), next(it)
    if state:
        k32_ref, v32_ref = next(it), next(it)

    mod = mod_ref[0]
    h = _modulate(x_ref[...], g_ref[...], mod[0:1], mod[1:2])
    qkv = _dot(h.astype(BF16), w_ref[...])
    nq = N_HEADS * HEAD_DIM
    nk = N_KV_HEADS * HEAD_DIM
    chunks = [qkv[:, LANES * j:LANES * (j + 1)] for j in range((nq + nk) // LANES)]
    if qknorm:
        gains = [qn_ref[...]] * (nq // LANES) + [kn_ref[...]] * (nk // LANES)
        gm = gm_ref[...]
        chunks = [c * lax.rsqrt(_dot((c * c).astype(BF16), gm) + EPS) * g for c, g in zip(chunks, gains)]
    k_state = jnp.concatenate(chunks[nq // LANES:], axis=1)
    if rope:
        cos, s1, s2 = cos_ref[...], s1_ref[...], s2_ref[...]
        chunks = [_rope_chunk(c, cos, s1, s2, HEAD_DIM // 2) for c in chunks]
    q = jnp.concatenate(chunks[:nq // LANES], axis=1) * (HEAD_DIM ** -0.5 * LOG2_E)
    k = jnp.concatenate(chunks[nq // LANES:], axis=1)
    v = qkv[:, nq + nk:]
    q_ref[...] = q.astype(BF16)
    kp_ref[...] = _head_layout(k, 0.0).astype(BF16)
    vp_ref[...] = _head_layout(v, 1.0).astype(BF16)
    if state:
        k32_ref[...] = k_state
        v32_ref[...] = v


def _proj_gqa(x, mods, gain, w, qn, kn, tables, *, latent, tm):
    n = x.shape[0]
    qknorm = qn is not None
    tiles_per_seq = 2048 // tm
    ins = [x, mods, gain.reshape(1, D_MODEL), w]
    specs = [
        pl.BlockSpec((tm, D_MODEL), lambda i: (i, 0)),
        _mod_spec(tiles_per_seq if latent else 0),
        pl.BlockSpec((1, D_MODEL), lambda i: (0, 0)),
        pl.BlockSpec((D_MODEL, QKV_DIM), lambda i: (0, 0)),
    ]
    if qknorm:
        gm = np.kron(np.eye(2), np.full((HEAD_DIM, HEAD_DIM), 1.0 / HEAD_DIM))
        ins += [jnp.tile(qn, 2).reshape(1, LANES), jnp.tile(kn, 2).reshape(1, LANES), jnp.asarray(gm, BF16)]
        specs += [pl.BlockSpec((1, LANES), lambda i: (0, 0))] * 2 + [pl.BlockSpec((LANES, LANES), lambda i: (0, 0))]
    if latent:
        ins += list(tables)
        specs += [pl.BlockSpec((tm, LANES), lambda i: (i % tiles_per_seq, 0))] * 3
    wide = N_HEADS * HEAD_DIM
    outs = [jax.ShapeDtypeStruct((n, wide), BF16)] * 3
    ospecs = [pl.BlockSpec((tm, wide), lambda i: (i, 0))] * 3
    if not latent:
        nk = N_KV_HEADS * HEAD_DIM
        outs += [jax.ShapeDtypeStruct((n, nk), F32)] * 2
        ospecs += [pl.BlockSpec((tm, nk), lambda i: (i, 0))] * 2
    return pl.pallas_call(
        functools.partial(_proj_gqa_kernel, qknorm=qknorm, rope=latent, state=not latent),
        grid=(n // tm,),
        in_specs=specs,
        out_specs=ospecs,
        out_shape=outs,
        compiler_params=_cparams(("parallel",)),
        name="proj_gqa_lat" if latent else "proj_gqa_ctx",
    )(*ins)


def _proj_mla_kernel(*refs, rope, state):
    it = iter(refs)
    x_ref, mod_ref, g_ref, wd_ref, qn_ref, wuq_ref, kvn_ref = (next(it) for _ in range(7))
    if rope:
        qt = [next(it) for _ in range(3)]
        kt = [next(it) for _ in range(3)]
    q_ref, ck_ref = next(it), next(it)
    if state:
        ckv_ref, kpe_ref = next(it), next(it)

    mod = mod_ref[0]
    h = _modulate(x_ref[...], g_ref[...], mod[0:1], mod[1:2])
    d = _dot(h.astype(BF16), wd_ref[...])
    cq = _rms(d[:, :MLA_Q_LORA], qn_ref[...])
    q = _dot(cq.astype(BF16), wuq_ref[...])
    ckv = _rms(d[:, MLA_Q_LORA:MLA_Q_LORA + MLA_KV_LORA], kvn_ref[...])
    kpe = d[:, MLA_Q_LORA + MLA_KV_LORA:]
    kpe_state = kpe
    scale = (MLA_NOPE + MLA_ROPE) ** -0.5 * LOG2_E
    qs = [q[:, LANES * j:LANES * (j + 1)] for j in range(MLA_HEADS)]
    if rope:
        cos, s1, s2 = (t[...] for t in qt)
        qs = [_rope_chunk(c, cos, s1, s2, MLA_ROPE // 2) for c in qs]
        kpe = _rope_chunk(kpe, *(t[...] for t in kt), MLA_ROPE // 2)
    q_ref[...] = (jnp.concatenate(qs, axis=1) * scale).astype(BF16)
    ones = lax.broadcasted_iota(jnp.int32, kpe.shape, 1) == MLA_ROPE
    ck_ref[...] = jnp.concatenate([ckv, jnp.where(ones, 1.0, kpe)], axis=1).astype(BF16)
    if state:
        ckv_ref[...] = ckv
        kpe_ref[...] = kpe_state[:, :MLA_ROPE]


def _proj_mla(x, mods, gain, wd, qn, wuq, kvn, qtables, ktables, *, latent, tm):
    n = x.shape[0]
    tiles_per_seq = 2048 // tm
    wq = MLA_HEADS * LANES
    ins = [x, mods, gain.reshape(1, D_MODEL), wd, qn.reshape(1, MLA_Q_LORA), wuq, kvn.reshape(1, MLA_KV_LORA)]
    specs = [
        pl.BlockSpec((tm, D_MODEL), lambda i: (i, 0)),
        _mod_spec(tiles_per_seq if latent else 0),
        pl.BlockSpec((1, D_MODEL), lambda i: (0, 0)),
        pl.BlockSpec((D_MODEL, MLA_DOWN), lambda i: (0, 0)),
        pl.BlockSpec((1, MLA_Q_LORA), lambda i: (0, 0)),
        pl.BlockSpec((MLA_Q_LORA, wq), lambda i: (0, 0)),
        pl.BlockSpec((1, MLA_KV_LORA), lambda i: (0, 0)),
    ]
    if latent:
        ins += list(qtables) + list(ktables)
        specs += [pl.BlockSpec((tm, LANES), lambda i: (i % tiles_per_seq, 0))] * 6
    outs = [jax.ShapeDtypeStruct((n, wq), BF16), jax.ShapeDtypeStruct((n, MLA_CK), BF16)]
    ospecs = [pl.BlockSpec((tm, wq), lambda i: (i, 0)), pl.BlockSpec((tm, MLA_CK), lambda i: (i, 0))]
    if not latent:
        outs += [jax.ShapeDtypeStruct((n, MLA_KV_LORA), F32), jax.ShapeDtypeStruct((n, MLA_ROPE), F32)]
        ospecs += [pl.BlockSpec((tm, MLA_KV_LORA), lambda i: (i, 0)), pl.BlockSpec((tm, MLA_ROPE), lambda i: (i, 0))]
    return pl.pallas_call(
        functools.partial(_proj_mla_kernel, rope=latent, state=not latent),
        grid=(n // tm,),
        in_specs=specs,
        out_specs=ospecs,
        out_shape=outs,
        compiler_params=_cparams(("parallel",)),
        name="proj_mla_lat" if latent else "proj_mla_ctx",
    )(*ins)


def _kv_expand_kernel(ck_ref, wk_ref, wv_ref, k_ref, v_ref):
    ck = ck_ref[...]
    k_ref[...] = _dot(ck, wk_ref[...]).astype(BF16)
    v_ref[...] = _dot(ck, wv_ref[...]).astype(BF16)


def _kv_expand(ck, wk, wv, tm):
    n = ck.shape[0]
    wide = MLA_HEADS * LANES
    return pl.pallas_call(
        _kv_expand_kernel,
        grid=(n // tm,),
        in_specs=[
            pl.BlockSpec((tm, MLA_CK), lambda i: (i, 0)),
            pl.BlockSpec((MLA_CK, wide), lambda i: (0, 0)),
            pl.BlockSpec((MLA_CK, wide), lambda i: (0, 0)),
        ],
        out_specs=[pl.BlockSpec((tm, wide), lambda i: (i, 0))] * 2,
        out_shape=[jax.ShapeDtypeStruct((n, wide), BF16)] * 2,
        compiler_params=_cparams(("parallel",)),
        name="mla_kv_expand",
    )(ck, wk, wv)


def _attn_kernel(*refs, units, n_seg, tq, band, use_sink, heads_per_step, mxu_denom):
    it = iter(refs)
    if use_sink:
        sink_ref = next(it)
    if band:
        bias_ref = next(it)
    q_ref = next(it)
    kv_refs = [(next(it), next(it)) for _ in range(n_seg)]
    o_ref = next(it)

    if band:
        qi = pl.program_id(2)
        seq = kv_refs[0][0].shape[0]
        win = tq + 2 * WINDOW
        start = pl.multiple_of(jnp.clip(qi * tq - WINDOW, 0, seq - win), WINDOW)

    for members, ke, ko in units:
        rows = tq * len(members)
        low_half = lax.broadcasted_iota(jnp.int32, (rows, LANES), 1) < HEAD_DIM
        halves = []
        for parity, koff in enumerate((ke, ko)):
            qh = jnp.concatenate([q_ref[:, mem[parity]:mem[parity] + LANES] for mem in members], axis=0)
            scores, values = [], []
            for si, (k_ref, v_ref) in enumerate(kv_refs):
                if band and si == 0:
                    kh = k_ref[pl.ds(start, win), koff:koff + LANES]
                    vh = v_ref[pl.ds(start, win), koff:koff + LANES]
                    s = _dot_nt(qh, kh) + jnp.concatenate([bias_ref[...]] * len(members), axis=0)
                else:
                    kh = k_ref[:, koff:koff + LANES]
                    vh = v_ref[:, koff:koff + LANES]
                    s = _dot_nt(qh, kh)
                scores.append(s)
                values.append(vh)
            m = functools.reduce(jnp.maximum, [jnp.max(s, axis=-1, keepdims=True) for s in scores])
            if use_sink:
                head0 = pl.program_id(1) * heads_per_step + parity
                sk = sink_ref[head0 + members[-1][3]] * LOG2_E
                row = lax.broadcasted_iota(jnp.int32, (rows, 1), 0)
                for i, mem in reversed(list(enumerate(members[:-1]))):
                    sk = jnp.where(row < tq * (i + 1), sink_ref[head0 + mem[3]] * LOG2_E, sk)
                m = jnp.maximum(m, sk)
            es = [jnp.exp2(s - m) for s in scores]
            pv = functools.reduce(jnp.add, [_dot(e.astype(BF16), vh) for e, vh in zip(es, values)])
            if mxu_denom:
                denom = pltpu.roll(pv, HEAD_DIM, 1)
            else:
                denom = functools.reduce(jnp.add, [jnp.sum(e, axis=-1, keepdims=True) for e in es])
            if use_sink:
                denom = denom + jnp.exp2(sk - m)
            halves.append(pv / denom)
        out = jnp.where(low_half, halves[0], halves[1]).astype(o_ref.dtype)
        for i, mem in enumerate(members):
            o_ref[:, mem[2]:mem[2] + LANES] = out[tq * i:tq * (i + 1)]


def _attention(q, segs, *, gqa, steps, tq, band=False, sink=None):
    b, sq, wq_total = q.shape
    wk_total = segs[0][0].shape[2]
    wo_total = N_HEADS * HEAD_DIM
    wq, wk, wo = wq_total // steps, wk_total // steps, wo_total // steps
    n_pairs = wo // LANES
    if gqa:
        pair = lambda p: (LANES * p, LANES * p, LANES * p, 2 * p)
        units = [([pair(2 * g), pair(2 * g + 1)], 2 * LANES * g, 2 * LANES * g + LANES) for g in range(n_pairs // 2)]
    else:
        pair = lambda p: (2 * LANES * p, 2 * LANES * p + LANES, LANES * p, 2 * p)
        units = [([pair(p)], 2 * LANES * p, 2 * LANES * p + LANES) for p in range(n_pairs)]
    ins, specs = [], []
    if sink is not None:
        ins.append(sink)
        specs.append(pl.BlockSpec(memory_space=pltpu.SMEM))
    if band:
        win, n_q = tq + 2 * WINDOW, sq // tq
        rel = np.arange(win)[None, :] - np.arange(tq)[:, None]
        bias = np.stack([np.where(np.abs(rel - off) <= WINDOW, 0.0, NEG_BIG) for off in (0, WINDOW, 2 * WINDOW)])
        ins.append(jnp.asarray(bias, F32))
        specs.append(pl.BlockSpec(
            (None, tq, win), lambda bi, si, qi: (jnp.where(qi == 0, 0, jnp.where(qi == n_q - 1, 2, 1)), 0, 0)))
    ins.append(q)
    specs.append(pl.BlockSpec((None, tq, wq), lambda bi, si, qi: (bi, qi, si)))
    for k, v in segs:
        sk = k.shape[1]
        ins += [k, v]
        specs += [pl.BlockSpec((None, sk, wk), lambda bi, si, qi: (bi, 0, si))] * 2
    return pl.pallas_call(
        functools.partial(_attn_kernel, units=units, n_seg=len(segs), tq=tq, band=band,
                          use_sink=sink is not None, heads_per_step=2 * n_pairs,
                          mxu_denom=sum(k.shape[1] for k, _ in segs) >= MXU_DENOM_MIN_KEYS),
        grid=(b, steps, sq // tq),
        in_specs=specs,
        out_specs=pl.BlockSpec((None, tq, wo), lambda bi, si, qi: (bi, qi, si)),
        out_shape=jax.ShapeDtypeStruct((b, sq, wo_total), BF16),
        compiler_params=_cparams(("parallel", "parallel", "parallel")),
        name="attention",
    )(*ins)


def _route(logits_t, bias_col):
    s = _sigmoid(logits_t)
    sb = s + bias_col
    r = [sb[e:e + 1, :] for e in range(N_EXPERTS)]
    sr = [s[e:e + 1, :] for e in range(N_EXPERTS)]
    gscore = []
    for g in range(N_GROUPS):
        a = r[EXPERTS_PER_GROUP * g:EXPERTS_PER_GROUP * (g + 1)]
        best = None
        for i in range(EXPERTS_PER_GROUP):
            for j in range(i + 1, EXPERTS_PER_GROUP):
                p = a[i] + a[j]
                best = p if best is None else jnp.maximum(best, p)
        gscore.append(best)
    gbest, gsel = gscore[0], jnp.zeros_like(gscore[0], dtype=jnp.int32)
    for g in range(1, N_GROUPS):
        better = gscore[g] > gbest
        gsel = jnp.where(better, g, gsel)
        gbest = jnp.where(better, gscore[g], gbest)
    neg = jnp.full_like(r[0], -jnp.inf)
    cand = [jnp.where(gsel == e // EXPERTS_PER_GROUP, r[e], neg) for e in range(N_EXPERTS)]

    def first_argmax(vals):
        best, idx, w = vals[0], jnp.zeros_like(gsel), sr[0]
        for e in range(1, N_EXPERTS):
            better = vals[e] > best
            idx = jnp.where(better, e, idx)
            w = jnp.where(better, sr[e], w)
            best = jnp.where(better, vals[e], best)
        return idx, w

    i1, w1 = first_argmax(cand)
    i2, w2 = first_argmax([jnp.where(i1 == e, neg, cand[e]) for e in range(N_EXPERTS)])
    tot = w1 + w2
    return i1, i2, w1 / tot, w2 / tot


def _moe_pre_kernel(x_ref, o_ref, mod_ref, g_ref, wo_ref, wr_hi_ref, wr_lo_ref, rb_ref, tri_ref,
                    xnew_ref, slot_ref, wts_ref, meta_ref):
    mod = mod_ref[0]
    xn = x_ref[...] + mod[2:3] * _dot(o_ref[...], wo_ref[...])
    xnew_ref[...] = xn
    h = _modulate(xn, g_ref[...], mod[3:4], mod[4:5])
    h_hi = h.astype(BF16)
    h_lo = (h - h_hi.astype(F32)).astype(BF16)
    logits_t = _dot_nt(wr_hi_ref[...], h_hi) + _dot_nt(wr_hi_ref[...], h_lo) + _dot_nt(wr_lo_ref[...], h_hi)
    i1, i2, g1, g2 = _route(logits_t, rb_ref[...])
    pad = jnp.zeros((LANES - 2, g1.shape[1]), F32)
    wts_ref[...] = jnp.concatenate([g1, g2, pad], axis=0).T

    rows = lax.broadcasted_iota(jnp.int32, logits_t.shape, 0)
    member = jnp.where((rows == i1) | (rows == i2), 1.0, 0.0)
    rank = _dot(member.astype(BF16), tri_ref[...])
    count = jnp.sum(member, axis=1, keepdims=True)
    lane = lax.broadcasted_iota(jnp.int32, (1, LANES), 1)
    start = jnp.zeros((1, 1), F32)
    slot1 = jnp.zeros_like(g1)
    slot2 = jnp.zeros_like(g1)
    starts = jnp.zeros((1, LANES), F32)
    chunks = jnp.zeros((1, LANES), F32)
    for e in range(N_EXPERTS):
        pos = start + rank[e:e + 1, :]
        slot1 = jnp.where(i1 == e, pos, slot1)
        slot2 = jnp.where(i2 == e, pos, slot2)
        cnt = count[e:e + 1, :]
        starts = jnp.where(lane == e, start, starts)
        chunks = jnp.where(lane == e, jnp.floor((cnt + (MOE_ROWS - 1)) * (1.0 / MOE_ROWS)), chunks)
        start = start + cnt
    sub = D_MODEL // LANES
    slot_ref[...] = slot1.astype(jnp.int32) * sub + slot2.astype(jnp.int32) * (sub << SLOT_BITS)
    meta_ref[...] = jnp.concatenate([starts, chunks], axis=0).astype(jnp.int32)


def _moe_pre(x, o, mods, gain, wo, wr_hi, wr_lo, rbias, *, latent):
    n = x.shape[0]
    tm = MOE_BLOCK
    const = lambda i: (0, 0)
    tri = jnp.asarray(np.triu(np.ones((tm, tm), np.float32), k=1), BF16)
    return pl.pallas_call(
        _moe_pre_kernel,
        grid=(n // tm,),
        in_specs=[
            pl.BlockSpec((tm, D_MODEL), lambda i: (i, 0)),
            pl.BlockSpec((tm, D_MODEL), lambda i: (i, 0)),
            _mod_spec((2048 // tm) if latent else 0),
            pl.BlockSpec((1, D_MODEL), const),
            pl.BlockSpec((D_MODEL, D_MODEL), const),
            pl.BlockSpec((LANES, D_MODEL), const),
            pl.BlockSpec((LANES, D_MODEL), const),
            pl.BlockSpec((LANES, 1), const),
            pl.BlockSpec((tm, tm), const),
        ],
        out_specs=[
            pl.BlockSpec((tm, D_MODEL), lambda i: (i, 0)),
            pl.BlockSpec((None, 1, tm), lambda i: (i, 0, 0)),
            pl.BlockSpec((tm, LANES), lambda i: (i, 0)),
            pl.BlockSpec((None, 2, LANES), lambda i: (i, 0, 0)),
        ],
        out_shape=[
            jax.ShapeDtypeStruct((n, D_MODEL), F32),
            jax.ShapeDtypeStruct((n // tm, 1, tm), jnp.int32),
            jax.ShapeDtypeStruct((n, LANES), F32),
            jax.ShapeDtypeStruct((n // tm, 2, LANES), jnp.int32),
        ],
        compiler_params=_cparams(("parallel",)),
        name="moe_pre_lat" if latent else "moe_pre_ctx",
    )(x, o, mods, gain.reshape(1, D_MODEL), wo, wr_hi, wr_lo, rbias, tri)


def _moe_kernel(slot_ref, start_ref, chunks_ref,
                xnew_ref, wts_ref, mod_ref, g_ref, wgu_ref, wd_ref, fin_ref, out_ref,
                tok_ref, tok2_ref, xs_ref, y_ref, *, tb, final):
    mask = (1 << SLOT_BITS) - 1
    b, e = pl.program_id(0), pl.program_id(1)
    tok0 = b * tb
    sub = D_MODEL // LANES

    @pl.when(e == 0)
    def _():
        @pl.when(b == 0)
        def _():
            xs_ref[2 * tb * sub:, :] = jnp.zeros((MOE_ROWS * sub, LANES), F32)

        mod = mod_ref[0]
        h = _modulate(xnew_ref[...], g_ref[...], mod[3:4], mod[4:5])
        for s in range(sub):
            tok_ref[pl.ds(s, tb, stride=sub), :] = h[:, LANES * s:LANES * (s + 1)]

        def dispatch(i, carry):
            for k in range(MOE_UNROLL):
                t = i * MOE_UNROLL + k
                row = tok_ref[pl.ds(pl.multiple_of(t * sub, sub), sub), :]
                packed = slot_ref[tok0 + t]
                xs_ref[pl.ds(pl.multiple_of(packed & mask, sub), sub), :] = row
                xs_ref[pl.ds(pl.multiple_of(packed >> SLOT_BITS, sub), sub), :] = row
            return carry
        lax.fori_loop(0, tb // MOE_UNROLL, dispatch, 0)

    def chunk(k, c):
        meta = b * N_EXPERTS + e * MOE_STEP_EXPERTS + k
        base = pl.multiple_of((start_ref[meta] + c * MOE_ROWS) * sub, sub)
        rows_in = xs_ref.at[pl.ds(base, MOE_ROWS * sub)]
        lhs = jnp.concatenate([rows_in[pl.ds(s, MOE_ROWS, stride=sub), :] for s in range(sub)], axis=1)
        au = _dot(lhs.astype(BF16), wgu_ref[k])
        a, u = au[:, :D_EXPERT], au[:, D_EXPERT:]
        y = _dot((a * _sigmoid(a) * u).astype(BF16), wd_ref[k])
        rows_out = y_ref.at[pl.ds(base, MOE_ROWS * sub)]
        for s in range(sub):
            rows_out[pl.ds(s, MOE_ROWS, stride=sub), :] = y[:, LANES * s:LANES * (s + 1)]

    n_chunks = [chunks_ref[b * N_EXPERTS + e * MOE_STEP_EXPERTS + k] for k in range(MOE_STEP_EXPERTS)]
    single = functools.reduce(jnp.logical_and, [n <= 1 for n in n_chunks])

    @pl.when(single)
    def _():
        for k in range(MOE_STEP_EXPERTS):
            chunk(k, 0)

    @pl.when(jnp.logical_not(single))
    def _():
        for k in range(MOE_STEP_EXPERTS):
            def body(c, carry, k=k):
                chunk(k, c)
                return carry
            lax.fori_loop(0, n_chunks[k], body, 0)

    @pl.when(e == N_EXPERTS // MOE_STEP_EXPERTS - 1)
    def _():
        def combine(i, carry):
            for k in range(MOE_UNROLL):
                t = i * MOE_UNROLL + k
                packed = slot_ref[tok0 + t]
                dst = pl.ds(pl.multiple_of(t * sub, sub), sub)
                tok_ref[dst, :] = y_ref[pl.ds(pl.multiple_of(packed & mask, sub), sub), :]
                tok2_ref[dst, :] = y_ref[pl.ds(pl.multiple_of(packed >> SLOT_BITS, sub), sub), :]
            return carry
        lax.fori_loop(0, tb // MOE_UNROLL, combine, 0)
        first = jnp.concatenate([tok_ref[pl.ds(s, tb, stride=sub), :] for s in range(sub)], axis=1)
        second = jnp.concatenate([tok2_ref[pl.ds(s, tb, stride=sub), :] for s in range(sub)], axis=1)
        wts = wts_ref[...]
        moe = wts[:, 0:1] * first + wts[:, 1:2] * second
        out = xnew_ref[...] + mod_ref[0][5:6] * moe
        if final:
            out = _rms(out, fin_ref[...])
        out_ref[...] = out


def _moe(xnew, slots, wts, meta, mods, gain, wgu, wd, fin, *, layer, latent, final):
    n = xnew.shape[0]
    tb = MOE_BLOCK
    cap = 2 * tb + MOE_ROWS
    sub = D_MODEL // LANES
    assert cap * sub < (1 << SLOT_BITS)
    const = lambda i, e: (0, 0)
    rows_per_cond = (2048 // tb) if latent else 0
    if rows_per_cond == 0:
        mod_spec = pl.BlockSpec((1, 6, D_MODEL), lambda i, e: (0, 0, 0))
    else:
        mod_spec = pl.BlockSpec((1, 6, D_MODEL), lambda i, e: (1 + i // rows_per_cond, 0, 0))
    smem = pl.BlockSpec(memory_space=pltpu.SMEM)
    return pl.pallas_call(
        functools.partial(_moe_kernel, tb=tb, final=final),
        grid=(n // tb, N_EXPERTS // MOE_STEP_EXPERTS),
        in_specs=[smem] * 3 + [
            pl.BlockSpec((tb, D_MODEL), lambda i, e: (i, 0)),
            pl.BlockSpec((tb, LANES), lambda i, e: (i, 0)),
            mod_spec,
            pl.BlockSpec((1, D_MODEL), const),
            pl.BlockSpec((None, MOE_STEP_EXPERTS, D_MODEL, 2 * D_EXPERT), lambda i, e: (layer, e, 0, 0)),
            pl.BlockSpec((None, MOE_STEP_EXPERTS, D_EXPERT, D_MODEL), lambda i, e: (layer, e, 0, 0)),
            pl.BlockSpec((1, D_MODEL), const),
        ],
        out_specs=pl.BlockSpec((tb, D_MODEL), lambda i, e: (i, 0)),
        out_shape=jax.ShapeDtypeStruct((n, D_MODEL), F32),
        scratch_shapes=[
            pltpu.VMEM((tb * sub, LANES), F32),
            pltpu.VMEM((tb * sub, LANES), F32),
            pltpu.VMEM((cap * sub, LANES), F32),
            pltpu.VMEM((cap * sub, LANES), F32),
        ],
        compiler_params=_cparams(("arbitrary", "arbitrary")),
        name="moe_lat" if latent else "moe_ctx",
    )(slots.reshape(n), meta[:, 0, :N_EXPERTS].reshape(-1), meta[:, 1, :N_EXPERTS].reshape(-1),
      xnew, wts, mods, gain.reshape(1, D_MODEL), wgu, wd, fin.reshape(1, D_MODEL))


def _expert_weights_kernel(wg_ref, wu_ref, wd_ref, wgu_ref, wdn_ref):
    wgu_ref[:, :D_EXPERT] = wg_ref[...].astype(BF16)
    wgu_ref[:, D_EXPERT:] = wu_ref[...].astype(BF16)
    wdn_ref[...] = wd_ref[...].astype(BF16)


def _expert_weights(w_gate, w_up, w_down):
    up = pl.BlockSpec((None, None, D_MODEL, D_EXPERT), lambda l, e: (l, e, 0, 0))
    down = pl.BlockSpec((None, None, D_EXPERT, D_MODEL), lambda l, e: (l, e, 0, 0))
    return pl.pallas_call(
        _expert_weights_kernel,
        grid=(DEPTH, N_EXPERTS),
        in_specs=[up, up, down],
        out_specs=[pl.BlockSpec((None, None, D_MODEL, 2 * D_EXPERT), lambda l, e: (l, e, 0, 0)), down],
        out_shape=[jax.ShapeDtypeStruct((DEPTH, N_EXPERTS, D_MODEL, 2 * D_EXPERT), BF16),
                   jax.ShapeDtypeStruct((DEPTH, N_EXPERTS, D_EXPERT, D_MODEL), BF16)],
        compiler_params=_cparams(("parallel", "parallel")),
        name="expert_weights",
    )(w_gate, w_up, w_down)


def _cache_head_layout(c, fill):
    z = jnp.full_like(c, fill)
    even = jnp.concatenate([c, z], axis=-1)
    odd = jnp.concatenate([z, c], axis=-1)
    return jnp.stack([even, odd], axis=3).reshape(c.shape[0], c.shape[1], 2 * N_KV_HEADS * LANES).astype(BF16)


def _mla_weights(wdq, wdkv, wuq, wukv):
    wd = jnp.concatenate([wdq, wdkv, jnp.zeros((D_MODEL, MLA_DOWN - MLA_Q_LORA - MLA_KV_LORA - MLA_ROPE), F32)], axis=1)
    dk = MLA_NOPE + MLA_ROPE
    wuq_p = jnp.pad(wuq.reshape(MLA_Q_LORA, MLA_HEADS, dk), ((0, 0), (0, 0), (0, LANES - dk)))
    wuq_p = wuq_p.reshape(MLA_Q_LORA, MLA_HEADS * LANES)
    kv = wukv.reshape(MLA_KV_LORA, MLA_HEADS, MLA_NOPE + MLA_V)
    k_nope, v = kv[..., :MLA_NOPE], kv[..., MLA_NOPE:]
    wk_top = jnp.pad(k_nope, ((0, 0), (0, 0), (0, LANES - MLA_NOPE))).reshape(MLA_KV_LORA, MLA_HEADS * LANES)
    place = np.zeros((MLA_CK - MLA_KV_LORA, MLA_HEADS, LANES), np.float32)
    for r in range(MLA_ROPE):
        place[r, :, MLA_NOPE + r] = 1.0
    wk = jnp.concatenate([wk_top, jnp.asarray(place.reshape(MLA_CK - MLA_KV_LORA, MLA_HEADS * LANES))], axis=0)
    z = jnp.zeros_like(v)
    v_even = jnp.concatenate([v, z], axis=-1)
    v_odd = jnp.concatenate([z, v], axis=-1)
    parity = (jnp.arange(MLA_HEADS) % 2 == 0)[None, :, None]
    wv_top = jnp.where(parity, v_even, v_odd).reshape(MLA_KV_LORA, MLA_HEADS * LANES)
    sums = np.zeros((MLA_CK - MLA_KV_LORA, MLA_HEADS, LANES), np.float32)
    sums[MLA_ROPE, 0::2, MLA_V:] = 1.0
    sums[MLA_ROPE, 1::2, :MLA_V] = 1.0
    wv = jnp.concatenate([wv_top, jnp.asarray(sums.reshape(MLA_CK - MLA_KV_LORA, MLA_HEADS * LANES))], axis=0)
    return wd.astype(BF16), wuq_p.astype(BF16), wk.astype(BF16), wv.astype(BF16)


def kernel(x_prompt, x_sample, cache_swa_k, cache_swa_v, cache_qkn_k, cache_qkn_v, cache_mla_ckv, cache_mla_kpe, c, c_ctx, w_mod, b_mod, norm_mix, norm_ffn, swa_wqkv, swa_wo, swa_sink, qkn_wqkv, qkn_wo, qkn_qnorm, qkn_knorm, mla_wdq, mla_qnorm, mla_wuq, mla_wdkv, mla_kvnorm, mla_wukv, mla_wo, w_router, router_bias, moe_w_gate, moe_w_up, moe_w_down, final_norm):
    nb, sc, _ = x_prompt.shape
    nd, sl, _ = x_sample.shape
    past = cache_swa_k.shape[2]
    cond = jnp.zeros((COND_ROWS, D_MODEL), F32).at[0].set(c_ctx).at[1:1 + nd].set(c)
    mods = _adaln(cond, w_mod, b_mod)

    gqa_tables = _rope_tables(sl, HEAD_DIM, 0, HEAD_DIM)
    mla_q_tables = _rope_tables(sl, MLA_ROPE, MLA_NOPE, 0)
    mla_k_tables = _rope_tables(sl, MLA_ROPE, 0, 0)

    wr_t = jnp.zeros((LANES, D_MODEL), F32).at[:N_EXPERTS].set(w_router.T)
    wr_hi = wr_t.astype(BF16)
    wr_lo = (wr_t - wr_hi.astype(F32)).astype(BF16)
    rbias = jnp.zeros((LANES, 1), F32).at[:N_EXPERTS, 0].set(router_bias)
    wgu, wdn = _expert_weights(moe_w_gate, moe_w_up, moe_w_down)

    xc = x_prompt.reshape(nb * sc, D_MODEL)
    xl = x_sample.reshape(nd * sl, D_MODEL)
    tm_c, tm_l = PROJ_TOKENS, PROJ_TOKENS
    states = {k: [] for k in ("swa_k", "swa_v", "qkn_k", "qkn_v", "ckv", "kpe")}
    for i in range(DEPTH):
        kind, j = i % N_MIXERS, i // N_MIXERS
        m = mods[i]
        if kind in (0, 1):
            if kind == 0:
                w, wo, qn, kn = swa_wqkv[j], swa_wo[j], None, None
                ck, cv, sink = cache_swa_k[:, j], cache_swa_v[:, j], swa_sink[j]
            else:
                w, wo, qn, kn = qkn_wqkv[j], qkn_wo[j], qkn_qnorm[j], qkn_knorm[j]
                ck, cv, sink = cache_qkn_k[:, j], cache_qkn_v[:, j], None
            w = w.astype(BF16)
            qc, kc, vc, k32, v32 = _proj_gqa(xc, m, norm_mix[i], w, qn, kn, None, latent=False, tm=tm_c)
            ql, kl, vl = _proj_gqa(xl, m, norm_mix[i], w, qn, kn, gqa_tables, latent=True, tm=tm_l)
            wide = N_HEADS * HEAD_DIM
            oc = _attention(qc.reshape(nb, sc, wide), [(kc.reshape(nb, sc, wide), vc.reshape(nb, sc, wide))],
                            gqa=True, steps=1, tq=sc, sink=sink)
            ol = _attention(ql.reshape(nd, sl, wide),
                            [(kl.reshape(nd, sl, wide), vl.reshape(nd, sl, wide)),
                             (_cache_head_layout(ck, 0.0), _cache_head_layout(cv, 1.0))],
                            gqa=True, steps=1, tq=ATTN_Q_BAND if kind == 0 else ATTN_Q,
                            band=(kind == 0), sink=sink)
            names = ("swa_k", "swa_v") if kind == 0 else ("qkn_k", "qkn_v")
            states[names[0]].append(k32.reshape(nb, sc, N_KV_HEADS, HEAD_DIM))
            states[names[1]].append(v32.reshape(nb, sc, N_KV_HEADS, HEAD_DIM))
        else:
            wd, wuq, wk, wv = _mla_weights(mla_wdq[j], mla_wdkv[j], mla_wuq[j], mla_wukv[j])
            wo = mla_wo[j]
            qc, ckc, ckv32, kpe32 = _proj_mla(xc, m, norm_mix[i], wd, mla_qnorm[j], wuq, mla_kvnorm[j],
                                              None, None, latent=False, tm=tm_c)
            ql, ckl = _proj_mla(xl, m, norm_mix[i], wd, mla_qnorm[j], wuq, mla_kvnorm[j],
                                mla_q_tables, mla_k_tables, latent=True, tm=tm_l)
            cache = jnp.concatenate(
                [cache_mla_ckv[:, j], cache_mla_kpe[:, j], jnp.ones((nd, past, 1), F32),
                 jnp.zeros((nd, past, MLA_CK - MLA_KV_LORA - MLA_ROPE - 1), F32)], axis=-1).astype(BF16)
            ck_all = jnp.concatenate([ckl.reshape(nd, sl, MLA_CK), cache], axis=1)
            wq = MLA_HEADS * LANES
            kc, vc = _kv_expand(ckc, wk, wv, tm_c)
            kl, vl = _kv_expand(ck_all.reshape(nd * (sl + past), MLA_CK), wk, wv, 512)
            oc = _attention(qc.reshape(nb, sc, wq), [(kc.reshape(nb, sc, wq), vc.reshape(nb, sc, wq))],
                            gqa=False, steps=1, tq=sc)
            ol = _attention(ql.reshape(nd, sl, wq), [(kl.reshape(nd, sl + past, wq), vl.reshape(nd, sl + past, wq))],
                            gqa=False, steps=MLA_HEADS // 8, tq=ATTN_Q_MLA)
            states["ckv"].append(ckv32.reshape(nb, sc, MLA_KV_LORA))
            states["kpe"].append(kpe32.reshape(nb, sc, MLA_ROPE))
        final = i == DEPTH - 1
        pre = (m, norm_ffn[i], wo.astype(BF16), wr_hi, wr_lo, rbias)
        post = (m, norm_ffn[i], wgu, wdn, final_norm)
        xc, slots, wts, meta = _moe_pre(xc, oc.reshape(nb * sc, D_MODEL), *pre, latent=False)
        xc = _moe(xc, slots, wts, meta, *post, layer=i, latent=False, final=final)
        xl, slots, wts, meta = _moe_pre(xl, ol.reshape(nd * sl, D_MODEL), *pre, latent=True)
        xl = _moe(xl, slots, wts, meta, *post, layer=i, latent=True, final=final)
    return (xc.reshape(nb, sc, D_MODEL), xl.reshape(nd, sl, D_MODEL),
            jnp.stack(states["swa_k"], axis=1), jnp.stack(states["swa_v"], axis=1),
            jnp.stack(states["qkn_k"], axis=1), jnp.stack(states["qkn_v"], axis=1),
            jnp.stack(states["ckv"], axis=1), jnp.stack(states["kpe"], axis=1))
```

```python
import functools

import numpy as np
import jax
import jax.numpy as jnp
from jax import lax
from jax.experimental import pallas as pl
from jax.experimental.pallas import tpu as pltpu

F32 = jnp.float32
BF16 = jnp.bfloat16

D_MODEL = 1024
DEPTH = 4
GRID_W = 64
N_MIXERS = 3
N_HEADS = 16
N_KV_HEADS = 4
HEAD_DIM = 64
QKV_DIM = (N_HEADS + 2 * N_KV_HEADS) * HEAD_DIM
WINDOW = 128
ROPE_THETA = 10000.0
MLA_HEADS = 16
MLA_Q_LORA = 384
MLA_KV_LORA = 256
MLA_NOPE = 64
MLA_ROPE = 32
MLA_V = 64
N_EXPERTS = 16
N_GROUPS = 4
EXPERTS_PER_GROUP = N_EXPERTS // N_GROUPS
D_EXPERT = 256
EPS = 1e-6

LANES = 128
COND_ROWS = 16
MLA_DOWN = 768
MLA_CK = 384
PROJ_TOKENS = 512
ATTN_Q = 256
ATTN_Q_MLA = 512
ATTN_Q_BAND = 256
MOE_BLOCK = 1024
MOE_ROWS = 192
MOE_STEP_EXPERTS = 2
SLOT_BITS = 16
MOE_UNROLL = 32
SPLIT_LANES = np.concatenate([np.arange(0, 32), np.arange(64, 96), np.arange(32, 64), np.arange(96, 128)])
NEG_BIG = -1e30
MXU_DENOM_MIN_KEYS = 1024
LOG2_E = 1.4426950408889634
VMEM_LIMIT = 56 * 1024 * 1024


def _cparams(sem):
    return pltpu.CompilerParams(dimension_semantics=sem, vmem_limit_bytes=VMEM_LIMIT)


def _sigmoid(x):
    return 1.0 / (1.0 + jnp.exp(-x))


def _rms(x, g):
    return x * lax.rsqrt(jnp.mean(x * x, axis=-1, keepdims=True) + EPS) * g


def _modulate(x, g, shift, scale):
    return x * lax.rsqrt(jnp.mean(x * x, axis=-1, keepdims=True) + EPS) * (g * (1.0 + scale)) + shift


def _dot(a, b):
    return jnp.dot(a, b, preferred_element_type=F32)


def _dot_nt(a, b):
    return lax.dot_general(a, b, (((1,), (1,)), ((), ())), preferred_element_type=F32)


def _adaln_kernel(cond_ref, w_ref, b_ref, o_ref):
    c = cond_ref[...]
    s = (c * _sigmoid(c)).astype(BF16)
    o_ref[...] = _dot(s, w_ref[...].astype(BF16)) + b_ref[...]


def _adaln(cond, w_mod, b_mod):
    tn = 1536
    n = 6 * D_MODEL
    out = pl.pallas_call(
        _adaln_kernel,
        grid=(DEPTH, n // tn),
        in_specs=[
            pl.BlockSpec((COND_ROWS, D_MODEL), lambda l, j: (0, 0)),
            pl.BlockSpec((None, D_MODEL, tn), lambda l, j: (l, 0, j)),
            pl.BlockSpec((None, 1, tn), lambda l, j: (l, 0, j)),
        ],
        out_specs=pl.BlockSpec((None, COND_ROWS, tn), lambda l, j: (l, 0, j)),
        out_shape=jax.ShapeDtypeStruct((DEPTH, COND_ROWS, n), F32),
        compiler_params=_cparams(("parallel", "parallel")),
        name="adaln",
    )(cond, w_mod, b_mod.reshape(DEPTH, 1, n))
    return out.reshape(DEPTH, COND_ROWS, 6, D_MODEL)


def _mod_spec(rows_per_cond):
    if rows_per_cond == 0:
        return pl.BlockSpec((1, 6, D_MODEL), lambda i: (0, 0, 0))
    return pl.BlockSpec((1, 6, D_MODEL), lambda i: (1 + i // rows_per_cond, 0, 0))


def _rope_tables(seq, head_dim, lane0, period):
    half, quarter = head_dim // 2, head_dim // 4
    pos = np.arange(seq)
    row, col = (pos // GRID_W).astype(np.float64), (pos % GRID_W).astype(np.float64)
    inv = ROPE_THETA ** (-np.arange(quarter, dtype=np.float64) / quarter)
    ang = np.concatenate([row[:, None] * inv, col[:, None] * inv], axis=-1)
    cos = np.ones((seq, LANES))
    s1 = np.zeros((seq, LANES))
    s2 = np.zeros((seq, LANES))
    starts = [lane0] if period == 0 else list(range(lane0, LANES, period))
    for st in starts:
        cos[:, st:st + half] = np.cos(ang)
        cos[:, st + half:st + head_dim] = np.cos(ang)
        s1[:, st:st + half] = -np.sin(ang)
        s2[:, st + half:st + head_dim] = np.sin(ang)
    return tuple(jnp.asarray(t, F32) for t in (cos, s1, s2))


def _rope_chunk(x, cos, s1, s2, half):
    return x * cos + pltpu.roll(x, LANES - half, 1) * s1 + pltpu.roll(x, half, 1) * s2


def _head_layout(kv, fill, split=False):
    lane = lax.broadcasted_iota(jnp.int32, (kv.shape[0], LANES), 1)
    half = HEAD_DIM // 2 if split else HEAD_DIM
    lo = (lane // half) % 2 == 0
    zero = jnp.full((kv.shape[0], LANES), fill, F32)
    out = []
    for j in range(2):
        ch = kv[:, LANES * j:LANES * (j + 1)]
        up, down = pltpu.roll(ch, half, 1), pltpu.roll(ch, LANES - half, 1)
        out += [jnp.where(lo, ch, zero), jnp.where(lo, zero, up), jnp.where(lo, down, zero), jnp.where(lo, zero, ch)]
    return jnp.concatenate(out, axis=1)


def _proj_gqa_kernel(*refs, qknorm, rope, state):
    it = iter(refs)
    x_ref, mod_ref, g_ref, w_ref = next(it), next(it), next(it), next(it)
    if qknorm:
        qn_ref, kn_ref, gm_ref = next(it), next(it), next(it)
    if rope:
        cos_ref, sin_ref = next(it), next(it)
    q_ref, kp_ref, vp_ref = next(it), next(it), next(it)
    if state:
        k32_ref, v32_ref = next(it), next(it)

    mod = mod_ref[0]
    h = _modulate(x_ref[...], g_ref[...], mod[0:1], mod[1:2])
    qkv = _dot(h.astype(BF16), w_ref[...])
    nq = N_HEADS * HEAD_DIM
    nk = N_KV_HEADS * HEAD_DIM
    chunks = [qkv[:, LANES * j:LANES * (j + 1)] for j in range((nq + nk) // LANES)]
    if qknorm:
        gains = [qn_ref[...]] * (nq // LANES) + [kn_ref[...]] * (nk // LANES)
        gm = gm_ref[...]
        chunks = [c * lax.rsqrt(_dot((c * c).astype(BF16), gm) + EPS) * g for c, g in zip(chunks, gains)]
    k_state = jnp.concatenate(chunks[nq // LANES:], axis=1)
    if rope:
        cos, sin = cos_ref[...], sin_ref[...]
        chunks = [c * cos + pltpu.roll(c, LANES // 2, 1) * sin for c in chunks]
    q = jnp.concatenate(chunks[:nq // LANES], axis=1) * (HEAD_DIM ** -0.5 * LOG2_E)
    k = jnp.concatenate(chunks[nq // LANES:], axis=1)
    v = qkv[:, nq + nk:]
    q_ref[...] = q.astype(BF16)
    kp_ref[...] = _head_layout(k, 0.0, split=rope).astype(BF16)
    vp_ref[...] = _head_layout(v, 1.0).astype(BF16)
    if state:
        k32_ref[...] = k_state
        v32_ref[...] = v


def _proj_gqa(x, mods, gain, w, qn, kn, tables, *, latent, tm):
    n = x.shape[0]
    qknorm = qn is not None
    tiles_per_seq = 2048 // tm
    order = SPLIT_LANES if latent else np.arange(LANES)
    if latent:
        nqk = (N_HEADS + N_KV_HEADS) * HEAD_DIM
        cols = np.concatenate([(LANES * (np.arange(nqk) // LANES) + np.tile(order, nqk // LANES)), np.arange(nqk, QKV_DIM)])
        w = w[:, cols]
    ins = [x, mods, gain.reshape(1, D_MODEL), w]
    specs = [
        pl.BlockSpec((tm, D_MODEL), lambda i: (i, 0)),
        _mod_spec(tiles_per_seq if latent else 0),
        pl.BlockSpec((1, D_MODEL), lambda i: (0, 0)),
        pl.BlockSpec((D_MODEL, QKV_DIM), lambda i: (0, 0)),
    ]
    if qknorm:
        head = order // HEAD_DIM
        gm = (head[:, None] == head[None, :]) / HEAD_DIM
        ins += [jnp.tile(qn, 2)[order].reshape(1, LANES), jnp.tile(kn, 2)[order].reshape(1, LANES), jnp.asarray(gm, BF16)]
        specs += [pl.BlockSpec((1, LANES), lambda i: (0, 0))] * 2 + [pl.BlockSpec((LANES, LANES), lambda i: (0, 0))]
    if latent:
        ins += list(tables)
        specs += [pl.BlockSpec((tm, LANES), lambda i: (i % tiles_per_seq, 0))] * 2
    wide = N_HEADS * HEAD_DIM
    outs = [jax.ShapeDtypeStruct((n, wide), BF16)] * 3
    ospecs = [pl.BlockSpec((tm, wide), lambda i: (i, 0))] * 3
    if not latent:
        nk = N_KV_HEADS * HEAD_DIM
        outs += [jax.ShapeDtypeStruct((n, nk), F32)] * 2
        ospecs += [pl.BlockSpec((tm, nk), lambda i: (i, 0))] * 2
    return pl.pallas_call(
        functools.partial(_proj_gqa_kernel, qknorm=qknorm, rope=latent, state=not latent),
        grid=(n // tm,),
        in_specs=specs,
        out_specs=ospecs,
        out_shape=outs,
        compiler_params=_cparams(("parallel",)),
        name="proj_gqa_lat" if latent else "proj_gqa_ctx",
    )(*ins)


def _proj_mla_kernel(*refs, rope, state):
    it = iter(refs)
    x_ref, mod_ref, g_ref, wd_ref, qn_ref, wuq_ref, kvn_ref = (next(it) for _ in range(7))
    if rope:
        qt = [next(it) for _ in range(3)]
        kt = [next(it) for _ in range(3)]
    q_ref, ck_ref = next(it), next(it)
    if state:
        ckv_ref, kpe_ref = next(it), next(it)

    mod = mod_ref[0]
    h = _modulate(x_ref[...], g_ref[...], mod[0:1], mod[1:2])
    d = _dot(h.astype(BF16), wd_ref[...])
    cq = _rms(d[:, :MLA_Q_LORA], qn_ref[...])
    q = _dot(cq.astype(BF16), wuq_ref[...])
    ckv = _rms(d[:, MLA_Q_LORA:MLA_Q_LORA + MLA_KV_LORA], kvn_ref[...])
    kpe = d[:, MLA_Q_LORA + MLA_KV_LORA:]
    kpe_state = kpe
    scale = (MLA_NOPE + MLA_ROPE) ** -0.5 * LOG2_E
    qs = [q[:, LANES * j:LANES * (j + 1)] for j in range(MLA_HEADS)]
    if rope:
        cos, s1, s2 = (t[...] for t in qt)
        qs = [_rope_chunk(c, cos, s1, s2, MLA_ROPE // 2) for c in qs]
        kpe = _rope_chunk(kpe, *(t[...] for t in kt), MLA_ROPE // 2)
    q_ref[...] = (jnp.concatenate(qs, axis=1) * scale).astype(BF16)
    ones = lax.broadcasted_iota(jnp.int32, kpe.shape, 1) == MLA_ROPE
    ck_ref[...] = jnp.concatenate([ckv, jnp.where(ones, 1.0, kpe)], axis=1).astype(BF16)
    if state:
        ckv_ref[...] = ckv
        kpe_ref[...] = kpe_state[:, :MLA_ROPE]


def _proj_mla(x, mods, gain, wd, qn, wuq, kvn, qtables, ktables, *, latent, tm):
    n = x.shape[0]
    tiles_per_seq = 2048 // tm
    wq = MLA_HEADS * LANES
    ins = [x, mods, gain.reshape(1, D_MODEL), wd, qn.reshape(1, MLA_Q_LORA), wuq, kvn.reshape(1, MLA_KV_LORA)]
    specs = [
        pl.BlockSpec((tm, D_MODEL), lambda i: (i, 0)),
        _mod_spec(tiles_per_seq if latent else 0),
        pl.BlockSpec((1, D_MODEL), lambda i: (0, 0)),
        pl.BlockSpec((D_MODEL, MLA_DOWN), lambda i: (0, 0)),
        pl.BlockSpec((1, MLA_Q_LORA), lambda i: (0, 0)),
        pl.BlockSpec((MLA_Q_LORA, wq), lambda i: (0, 0)),
        pl.BlockSpec((1, MLA_KV_LORA), lambda i: (0, 0)),
    ]
    if latent:
        ins += list(qtables) + list(ktables)
        specs += [pl.BlockSpec((tm, LANES), lambda i: (i % tiles_per_seq, 0))] * 6
    outs = [jax.ShapeDtypeStruct((n, wq), BF16), jax.ShapeDtypeStruct((n, MLA_CK), BF16)]
    ospecs = [pl.BlockSpec((tm, wq), lambda i: (i, 0)), pl.BlockSpec((tm, MLA_CK), lambda i: (i, 0))]
    if not latent:
        outs += [jax.ShapeDtypeStruct((n, MLA_KV_LORA), F32), jax.ShapeDtypeStruct((n, MLA_ROPE), F32)]
        ospecs += [pl.BlockSpec((tm, MLA_KV_LORA), lambda i: (i, 0)), pl.BlockSpec((tm, MLA_ROPE), lambda i: (i, 0))]
    return pl.pallas_call(
        functools.partial(_proj_mla_kernel, rope=latent, state=not latent),
        grid=(n // tm,),
        in_specs=specs,
        out_specs=ospecs,
        out_shape=outs,
        compiler_params=_cparams(("parallel",)),
        name="proj_mla_lat" if latent else "proj_mla_ctx",
    )(*ins)


def _kv_expand_kernel(ck_ref, wk_ref, wv_ref, k_ref, v_ref):
    ck = ck_ref[...]
    k_ref[...] = _dot(ck, wk_ref[...]).astype(BF16)
    v_ref[...] = _dot(ck, wv_ref[...]).astype(BF16)


def _kv_expand(ck, wk, wv, tm):
    n = ck.shape[0]
    wide = MLA_HEADS * LANES
    return pl.pallas_call(
        _kv_expand_kernel,
        grid=(n // tm,),
        in_specs=[
            pl.BlockSpec((tm, MLA_CK), lambda i: (i, 0)),
            pl.BlockSpec((MLA_CK, wide), lambda i: (0, 0)),
            pl.BlockSpec((MLA_CK, wide), lambda i: (0, 0)),
        ],
        out_specs=[pl.BlockSpec((tm, wide), lambda i: (i, 0))] * 2,
        out_shape=[jax.ShapeDtypeStruct((n, wide), BF16)] * 2,
        compiler_params=_cparams(("parallel",)),
        name="mla_kv_expand",
    )(ck, wk, wv)


def _attn_kernel(*refs, units, n_seg, tq, band, use_sink, heads_per_step, mxu_denom):
    it = iter(refs)
    if use_sink:
        sink_ref = next(it)
    if band:
        bias_ref = next(it)
    q_ref = next(it)
    kv_refs = [(next(it), next(it)) for _ in range(n_seg)]
    o_ref = next(it)

    if band:
        qi = pl.program_id(2)
        seq = kv_refs[0][0].shape[0]
        win = tq + 2 * WINDOW
        start = pl.multiple_of(jnp.clip(qi * tq - WINDOW, 0, seq - win), WINDOW)

    for members, ke, ko in units:
        rows = tq * len(members)
        low_half = lax.broadcasted_iota(jnp.int32, (rows, LANES), 1) < HEAD_DIM
        halves = []
        for parity, koff in enumerate((ke, ko)):
            qh = jnp.concatenate([q_ref[:, mem[parity]:mem[parity] + LANES] for mem in members], axis=0)
            scores, values = [], []
            for si, (k_ref, v_ref) in enumerate(kv_refs):
                if band and si == 0:
                    kh = k_ref[pl.ds(start, win), koff:koff + LANES]
                    vh = v_ref[pl.ds(start, win), koff:koff + LANES]
                    s = _dot_nt(qh, kh) + jnp.concatenate([bias_ref[...]] * len(members), axis=0)
                else:
                    kh = k_ref[:, koff:koff + LANES]
                    vh = v_ref[:, koff:koff + LANES]
                    s = _dot_nt(qh, kh)
                scores.append(s)
                values.append(vh)
            m = functools.reduce(jnp.maximum, [jnp.max(s, axis=-1, keepdims=True) for s in scores])
            if use_sink:
                head0 = pl.program_id(1) * heads_per_step + parity
                sk = sink_ref[head0 + members[-1][3]] * LOG2_E
                row = lax.broadcasted_iota(jnp.int32, (rows, 1), 0)
                for i, mem in reversed(list(enumerate(members[:-1]))):
                    sk = jnp.where(row < tq * (i + 1), sink_ref[head0 + mem[3]] * LOG2_E, sk)
                m = jnp.maximum(m, sk)
            es = [jnp.exp2(s - m) for s in scores]
            pv = functools.reduce(jnp.add, [_dot(e.astype(BF16), vh) for e, vh in zip(es, values)])
            if mxu_denom:
                denom = pltpu.roll(pv, HEAD_DIM, 1)
            else:
                denom = functools.reduce(jnp.add, [jnp.sum(e, axis=-1, keepdims=True) for e in es])
            if use_sink:
                denom = denom + jnp.exp2(sk - m)
            halves.append(pv / denom)
        out = jnp.where(low_half, halves[0], halves[1]).astype(o_ref.dtype)
        for i, mem in enumerate(members):
            o_ref[:, mem[2]:mem[2] + LANES] = out[tq * i:tq * (i + 1)]


def _attention(q, segs, *, gqa, steps, tq, band=False, sink=None):
    b, sq, wq_total = q.shape
    wk_total = segs[0][0].shape[2]
    wo_total = N_HEADS * HEAD_DIM
    wq, wk, wo = wq_total // steps, wk_total // steps, wo_total // steps
    n_pairs = wo // LANES
    if gqa:
        pair = lambda p: (LANES * p, LANES * p, LANES * p, 2 * p)
        units = [([pair(2 * g), pair(2 * g + 1)], 2 * LANES * g, 2 * LANES * g + LANES) for g in range(n_pairs // 2)]
    else:
        pair = lambda p: (2 * LANES * p, 2 * LANES * p + LANES, LANES * p, 2 * p)
        units = [([pair(p)], 2 * LANES * p, 2 * LANES * p + LANES) for p in range(n_pairs)]
    ins, specs = [], []
    if sink is not None:
        ins.append(sink)
        specs.append(pl.BlockSpec(memory_space=pltpu.SMEM))
    if band:
        win, n_q = tq + 2 * WINDOW, sq // tq
        rel = np.arange(win)[None, :] - np.arange(tq)[:, None]
        bias = np.stack([np.where(np.abs(rel - off) <= WINDOW, 0.0, NEG_BIG) for off in (0, WINDOW, 2 * WINDOW)])
        ins.append(jnp.asarray(bias, F32))
        specs.append(pl.BlockSpec(
            (None, tq, win), lambda bi, si, qi: (jnp.where(qi == 0, 0, jnp.where(qi == n_q - 1, 2, 1)), 0, 0)))
    ins.append(q)
    specs.append(pl.BlockSpec((None, tq, wq), lambda bi, si, qi: (bi, qi, si)))
    for k, v in segs:
        sk = k.shape[1]
        ins += [k, v]
        specs += [pl.BlockSpec((None, sk, wk), lambda bi, si, qi: (bi, 0, si))] * 2
    return pl.pallas_call(
        functools.partial(_attn_kernel, units=units, n_seg=len(segs), tq=tq, band=band,
                          use_sink=sink is not None, heads_per_step=2 * n_pairs,
                          mxu_denom=sum(k.shape[1] for k, _ in segs) >= MXU_DENOM_MIN_KEYS),
        grid=(b, steps, sq // tq),
        in_specs=specs,
        out_specs=pl.BlockSpec((None, tq, wo), lambda bi, si, qi: (bi, qi, si)),
        out_shape=jax.ShapeDtypeStruct((b, sq, wo_total), BF16),
        compiler_params=_cparams(("parallel", "parallel", "parallel")),
        name="attention",
    )(*ins)


def _route(logits_t, bias_col):
    s = _sigmoid(logits_t)
    sb = s + bias_col
    r = [sb[e:e + 1, :] for e in range(N_EXPERTS)]
    sr = [s[e:e + 1, :] for e in range(N_EXPERTS)]
    gscore = []
    for g in range(N_GROUPS):
        a = r[EXPERTS_PER_GROUP * g:EXPERTS_PER_GROUP * (g + 1)]
        best = None
        for i in range(EXPERTS_PER_GROUP):
            for j in range(i + 1, EXPERTS_PER_GROUP):
                p = a[i] + a[j]
                best = p if best is None else jnp.maximum(best, p)
        gscore.append(best)
    gbest, gsel = gscore[0], jnp.zeros_like(gscore[0], dtype=jnp.int32)
    for g in range(1, N_GROUPS):
        better = gscore[g] > gbest
        gsel = jnp.where(better, g, gsel)
        gbest = jnp.where(better, gscore[g], gbest)
    neg = jnp.full_like(r[0], -jnp.inf)
    cand = [jnp.where(gsel == e // EXPERTS_PER_GROUP, r[e], neg) for e in range(N_EXPERTS)]

    def first_argmax(vals):
        best, idx, w = vals[0], jnp.zeros_like(gsel), sr[0]
        for e in range(1, N_EXPERTS):
            better = vals[e] > best
            idx = jnp.where(better, e, idx)
            w = jnp.where(better, sr[e], w)
            best = jnp.where(better, vals[e], best)
        return idx, w

    i1, w1 = first_argmax(cand)
    i2, w2 = first_argmax([jnp.where(i1 == e, neg, cand[e]) for e in range(N_EXPERTS)])
    tot = w1 + w2
    return i1, i2, w1 / tot, w2 / tot


def _moe_pre_kernel(x_ref, o_ref, mod_ref, g_ref, wo_ref, wr_hi_ref, wr_lo_ref, rb_ref, tri_ref,
                    xnew_ref, slot_ref, wts_ref, meta_ref):
    mod = mod_ref[0]
    xn = x_ref[...] + mod[2:3] * _dot(o_ref[...], wo_ref[...])
    xnew_ref[...] = xn
    h = _modulate(xn, g_ref[...], mod[3:4], mod[4:5])
    h_hi = h.astype(BF16)
    h_lo = (h - h_hi.astype(F32)).astype(BF16)
    logits_t = _dot_nt(wr_hi_ref[...], h_hi) + _dot_nt(wr_hi_ref[...], h_lo) + _dot_nt(wr_lo_ref[...], h_hi)
    i1, i2, g1, g2 = _route(logits_t, rb_ref[...])
    pad = jnp.zeros((LANES - 2, g1.shape[1]), F32)
    wts_ref[...] = jnp.concatenate([g1, g2, pad], axis=0).T

    rows = lax.broadcasted_iota(jnp.int32, logits_t.shape, 0)
    member = jnp.where((rows == i1) | (rows == i2), 1.0, 0.0)
    rank = _dot(member.astype(BF16), tri_ref[...])
    count = jnp.sum(member, axis=1, keepdims=True)
    lane = lax.broadcasted_iota(jnp.int32, (1, LANES), 1)
    start = jnp.zeros((1, 1), F32)
    slot1 = jnp.zeros_like(g1)
    slot2 = jnp.zeros_like(g1)
    starts = jnp.zeros((1, LANES), F32)
    chunks = jnp.zeros((1, LANES), F32)
    for e in range(N_EXPERTS):
        pos = start + rank[e:e + 1, :]
        slot1 = jnp.where(i1 == e, pos, slot1)
        slot2 = jnp.where(i2 == e, pos, slot2)
        cnt = count[e:e + 1, :]
        starts = jnp.where(lane == e, start, starts)
        chunks = jnp.where(lane == e, jnp.floor((cnt + (MOE_ROWS - 1)) * (1.0 / MOE_ROWS)), chunks)
        start = start + cnt
    sub = D_MODEL // LANES
    slot_ref[...] = slot1.astype(jnp.int32) * sub + slot2.astype(jnp.int32) * (sub << SLOT_BITS)
    meta_ref[...] = jnp.concatenate([starts, chunks], axis=0).astype(jnp.int32)


def _moe_pre(x, o, mods, gain, wo, wr_hi, wr_lo, rbias, *, latent):
    n = x.shape[0]
    tm = MOE_BLOCK
    const = lambda i: (0, 0)
    tri = jnp.asarray(np.triu(np.ones((tm, tm), np.float32), k=1), BF16)
    return pl.pallas_call(
        _moe_pre_kernel,
        grid=(n // tm,),
        in_specs=[
            pl.BlockSpec((tm, D_MODEL), lambda i: (i, 0)),
            pl.BlockSpec((tm, D_MODEL), lambda i: (i, 0)),
            _mod_spec((2048 // tm) if latent else 0),
            pl.BlockSpec((1, D_MODEL), const),
            pl.BlockSpec((D_MODEL, D_MODEL), const),
            pl.BlockSpec((LANES, D_MODEL), const),
            pl.BlockSpec((LANES, D_MODEL), const),
            pl.BlockSpec((LANES, 1), const),
            pl.BlockSpec((tm, tm), const),
        ],
        out_specs=[
            pl.BlockSpec((tm, D_MODEL), lambda i: (i, 0)),
            pl.BlockSpec((None, 1, tm), lambda i: (i, 0, 0)),
            pl.BlockSpec((tm, LANES), lambda i: (i, 0)),
            pl.BlockSpec((None, 2, LANES), lambda i: (i, 0, 0)),
        ],
        out_shape=[
            jax.ShapeDtypeStruct((n, D_MODEL), F32),
            jax.ShapeDtypeStruct((n // tm, 1, tm), jnp.int32),
            jax.ShapeDtypeStruct((n, LANES), F32),
            jax.ShapeDtypeStruct((n // tm, 2, LANES), jnp.int32),
        ],
        compiler_params=_cparams(("parallel",)),
        name="moe_pre_lat" if latent else "moe_pre_ctx",
    )(x, o, mods, gain.reshape(1, D_MODEL), wo, wr_hi, wr_lo, rbias, tri)


def _moe_kernel(slot_ref, start_ref, chunks_ref,
                xnew_ref, wts_ref, mod_ref, g_ref, wgu_ref, wd_ref, fin_ref, out_ref,
                tok_ref, tok2_ref, xs_ref, y_ref, *, tb, final):
    mask = (1 << SLOT_BITS) - 1
    b, e = pl.program_id(0), pl.program_id(1)
    tok0 = b * tb
    sub = D_MODEL // LANES

    @pl.when(e == 0)
    def _():
        @pl.when(b == 0)
        def _():
            xs_ref[2 * tb * sub:, :] = jnp.zeros((MOE_ROWS * sub, LANES), F32)

        mod = mod_ref[0]
        h = _modulate(xnew_ref[...], g_ref[...], mod[3:4], mod[4:5])
        for s in range(sub):
            tok_ref[pl.ds(s, tb, stride=sub), :] = h[:, LANES * s:LANES * (s + 1)]

        def dispatch(i, carry):
            for k in range(MOE_UNROLL):
                t = i * MOE_UNROLL + k
                row = tok_ref[pl.ds(pl.multiple_of(t * sub, sub), sub), :]
                packed = slot_ref[tok0 + t]
                xs_ref[pl.ds(pl.multiple_of(packed & mask, sub), sub), :] = row
                xs_ref[pl.ds(pl.multiple_of(packed >> SLOT_BITS, sub), sub), :] = row
            return carry
        lax.fori_loop(0, tb // MOE_UNROLL, dispatch, 0)

    def chunk(k, c):
        meta = b * N_EXPERTS + e * MOE_STEP_EXPERTS + k
        base = pl.multiple_of((start_ref[meta] + c * MOE_ROWS) * sub, sub)
        rows_in = xs_ref.at[pl.ds(base, MOE_ROWS * sub)]
        lhs = jnp.concatenate([rows_in[pl.ds(s, MOE_ROWS, stride=sub), :] for s in range(sub)], axis=1)
        au = _dot(lhs.astype(BF16), wgu_ref[k])
        a, u = au[:, :D_EXPERT], au[:, D_EXPERT:]
        y = _dot((a * _sigmoid(a) * u).astype(BF16), wd_ref[k])
        rows_out = y_ref.at[pl.ds(base, MOE_ROWS * sub)]
        for s in range(sub):
            rows_out[pl.ds(s, MOE_ROWS, stride=sub), :] = y[:, LANES * s:LANES * (s + 1)]

    n_chunks = [chunks_ref[b * N_EXPERTS + e * MOE_STEP_EXPERTS + k] for k in range(MOE_STEP_EXPERTS)]
    single = functools.reduce(jnp.logical_and, [n <= 1 for n in n_chunks])

    @pl.when(single)
    def _():
        for k in range(MOE_STEP_EXPERTS):
            chunk(k, 0)

    @pl.when(jnp.logical_not(single))
    def _():
        for k in range(MOE_STEP_EXPERTS):
            def body(c, carry, k=k):
                chunk(k, c)
                return carry
            lax.fori_loop(0, n_chunks[k], body, 0)

    @pl.when(e == N_EXPERTS // MOE_STEP_EXPERTS - 1)
    def _():
        def combine(i, carry):
            for k in range(MOE_UNROLL):
                t = i * MOE_UNROLL + k
                packed = slot_ref[tok0 + t]
                dst = pl.ds(pl.multiple_of(t * sub, sub), sub)
                tok_ref[dst, :] = y_ref[pl.ds(pl.multiple_of(packed & mask, sub), sub), :]
                tok2_ref[dst, :] = y_ref[pl.ds(pl.multiple_of(packed >> SLOT_BITS, sub), sub), :]
            return carry
        lax.fori_loop(0, tb // MOE_UNROLL, combine, 0)
        first = jnp.concatenate([tok_ref[pl.ds(s, tb, stride=sub), :] for s in range(sub)], axis=1)
        second = jnp.concatenate([tok2_ref[pl.ds(s, tb, stride=sub), :] for s in range(sub)], axis=1)
        wts = wts_ref[...]
        moe = wts[:, 0:1] * first + wts[:, 1:2] * second
        out = xnew_ref[...] + mod_ref[0][5:6] * moe
        if final:
            out = _rms(out, fin_ref[...])
        out_ref[...] = out


def _moe(xnew, slots, wts, meta, mods, gain, wgu, wd, fin, *, layer, latent, final):
    n = xnew.shape[0]
    tb = MOE_BLOCK
    cap = 2 * tb + MOE_ROWS
    sub = D_MODEL // LANES
    assert cap * sub < (1 << SLOT_BITS)
    const = lambda i, e: (0, 0)
    rows_per_cond = (2048 // tb) if latent else 0
    if rows_per_cond == 0:
        mod_spec = pl.BlockSpec((1, 6, D_MODEL), lambda i, e: (0, 0, 0))
    else:
        mod_spec = pl.BlockSpec((1, 6, D_MODEL), lambda i, e: (1 + i // rows_per_cond, 0, 0))
    smem = pl.BlockSpec(memory_space=pltpu.SMEM)
    return pl.pallas_call(
        functools.partial(_moe_kernel, tb=tb, final=final),
        grid=(n // tb, N_EXPERTS // MOE_STEP_EXPERTS),
        in_specs=[smem] * 3 + [
            pl.BlockSpec((tb, D_MODEL), lambda i, e: (i, 0)),
            pl.BlockSpec((tb, LANES), lambda i, e: (i, 0)),
            mod_spec,
            pl.BlockSpec((1, D_MODEL), const),
            pl.BlockSpec((None, MOE_STEP_EXPERTS, D_MODEL, 2 * D_EXPERT), lambda i, e: (layer, e, 0, 0)),
            pl.BlockSpec((None, MOE_STEP_EXPERTS, D_EXPERT, D_MODEL), lambda i, e: (layer, e, 0, 0)),
            pl.BlockSpec((1, D_MODEL), const),
        ],
        out_specs=pl.BlockSpec((tb, D_MODEL), lambda i, e: (i, 0)),
        out_shape=jax.ShapeDtypeStruct((n, D_MODEL), F32),
        scratch_shapes=[
            pltpu.VMEM((tb * sub, LANES), F32),
            pltpu.VMEM((tb * sub, LANES), F32),
            pltpu.VMEM((cap * sub, LANES), F32),
            pltpu.VMEM((cap * sub, LANES), F32),
        ],
        compiler_params=_cparams(("arbitrary", "arbitrary")),
        name="moe_lat" if latent else "moe_ctx",
    )(slots.reshape(n), meta[:, 0, :N_EXPERTS].reshape(-1), meta[:, 1, :N_EXPERTS].reshape(-1),
      xnew, wts, mods, gain.reshape(1, D_MODEL), wgu, wd, fin.reshape(1, D_MODEL))


def _expert_weights_kernel(wg_ref, wu_ref, wd_ref, wgu_ref, wdn_ref):
    wgu_ref[:, :D_EXPERT] = wg_ref[...].astype(BF16)
    wgu_ref[:, D_EXPERT:] = wu_ref[...].astype(BF16)
    wdn_ref[...] = wd_ref[...].astype(BF16)


def _expert_weights(w_gate, w_up, w_down):
    up = pl.BlockSpec((None, None, D_MODEL, D_EXPERT), lambda l, e: (l, e, 0, 0))
    down = pl.BlockSpec((None, None, D_EXPERT, D_MODEL), lambda l, e: (l, e, 0, 0))
    return pl.pallas_call(
        _expert_weights_kernel,
        grid=(DEPTH, N_EXPERTS),
        in_specs=[up, up, down],
        out_specs=[pl.BlockSpec((None, None, D_MODEL, 2 * D_EXPERT), lambda l, e: (l, e, 0, 0)), down],
        out_shape=[jax.ShapeDtypeStruct((DEPTH, N_EXPERTS, D_MODEL, 2 * D_EXPERT), BF16),
                   jax.ShapeDtypeStruct((DEPTH, N_EXPERTS, D_EXPERT, D_MODEL), BF16)],
        compiler_params=_cparams(("parallel", "parallel")),
        name="expert_weights",
    )(w_gate, w_up, w_down)


def _cache_head_layout(c, fill, split=False):
    z = jnp.full_like(c, fill)
    even = jnp.concatenate([c, z], axis=-1)
    odd = jnp.concatenate([z, c], axis=-1)
    if split:
        even, odd = even[..., SPLIT_LANES], odd[..., SPLIT_LANES]
    return jnp.stack([even, odd], axis=3).reshape(c.shape[0], c.shape[1], 2 * N_KV_HEADS * LANES).astype(BF16)


def _mla_weights(wdq, wdkv, wuq, wukv):
    wd = jnp.concatenate([wdq, wdkv, jnp.zeros((D_MODEL, MLA_DOWN - MLA_Q_LORA - MLA_KV_LORA - MLA_ROPE), F32)], axis=1)
    dk = MLA_NOPE + MLA_ROPE
    wuq_p = jnp.pad(wuq.reshape(MLA_Q_LORA, MLA_HEADS, dk), ((0, 0), (0, 0), (0, LANES - dk)))
    wuq_p = wuq_p.reshape(MLA_Q_LORA, MLA_HEADS * LANES)
    kv = wukv.reshape(MLA_KV_LORA, MLA_HEADS, MLA_NOPE + MLA_V)
    k_nope, v = kv[..., :MLA_NOPE], kv[..., MLA_NOPE:]
    wk_top = jnp.pad(k_nope, ((0, 0), (0, 0), (0, LANES - MLA_NOPE))).reshape(MLA_KV_LORA, MLA_HEADS * LANES)
    place = np.zeros((MLA_CK - MLA_KV_LORA, MLA_HEADS, LANES), np.float32)
    for r in range(MLA_ROPE):
        place[r, :, MLA_NOPE + r] = 1.0
    wk = jnp.concatenate([wk_top, jnp.asarray(place.reshape(MLA_CK - MLA_KV_LORA, MLA_HEADS * LANES))], axis=0)
    z = jnp.zeros_like(v)
    v_even = jnp.concatenate([v, z], axis=-1)
    v_odd = jnp.concatenate([z, v], axis=-1)
    parity = (jnp.arange(MLA_HEADS) % 2 == 0)[None, :, None]
    wv_top = jnp.where(parity, v_even, v_odd).reshape(MLA_KV_LORA, MLA_HEADS * LANES)
    sums = np.zeros((MLA_CK - MLA_KV_LORA, MLA_HEADS, LANES), np.float32)
    sums[MLA_ROPE, 0::2, MLA_V:] = 1.0
    sums[MLA_ROPE, 1::2, :MLA_V] = 1.0
    wv = jnp.concatenate([wv_top, jnp.asarray(sums.reshape(MLA_CK - MLA_KV_LORA, MLA_HEADS * LANES))], axis=0)
    return wd.astype(BF16), wuq_p.astype(BF16), wk.astype(BF16), wv.astype(BF16)


def kernel(x_prompt, x_sample, cache_swa_k, cache_swa_v, cache_qkn_k, cache_qkn_v, cache_mla_ckv, cache_mla_kpe, c, c_ctx, w_mod, b_mod, norm_mix, norm_ffn, swa_wqkv, swa_wo, swa_sink, qkn_wqkv, qkn_wo, qkn_qnorm, qkn_knorm, mla_wdq, mla_qnorm, mla_wuq, mla_wdkv, mla_kvnorm, mla_wukv, mla_wo, w_router, router_bias, moe_w_gate, moe_w_up, moe_w_down, final_norm):
    nb, sc, _ = x_prompt.shape
    nd, sl, _ = x_sample.shape
    past = cache_swa_k.shape[2]
    cond = jnp.zeros((COND_ROWS, D_MODEL), F32).at[0].set(c_ctx).at[1:1 + nd].set(c)
    mods = _adaln(cond, w_mod, b_mod)

    cos, s1, s2 = _rope_tables(sl, HEAD_DIM, 0, HEAD_DIM)
    gqa_tables = (cos[:, SPLIT_LANES], (s1 + s2)[:, SPLIT_LANES])
    mla_q_tables = _rope_tables(sl, MLA_ROPE, MLA_NOPE, 0)
    mla_k_tables = _rope_tables(sl, MLA_ROPE, 0, 0)

    wr_t = jnp.zeros((LANES, D_MODEL), F32).at[:N_EXPERTS].set(w_router.T)
    wr_hi = wr_t.astype(BF16)
    wr_lo = (wr_t - wr_hi.astype(F32)).astype(BF16)
    rbias = jnp.zeros((LANES, 1), F32).at[:N_EXPERTS, 0].set(router_bias)
    wgu, wdn = _expert_weights(moe_w_gate, moe_w_up, moe_w_down)

    xc = x_prompt.reshape(nb * sc, D_MODEL)
    xl = x_sample.reshape(nd * sl, D_MODEL)
    tm_c, tm_l = PROJ_TOKENS, 2 * PROJ_TOKENS
    states = {k: [] for k in ("swa_k", "swa_v", "qkn_k", "qkn_v", "ckv", "kpe")}
    for i in range(DEPTH):
        kind, j = i % N_MIXERS, i // N_MIXERS
        m = mods[i]
        if kind in (0, 1):
            if kind == 0:
                w, wo, qn, kn = swa_wqkv[j], swa_wo[j], None, None
                ck, cv, sink = cache_swa_k[:, j], cache_swa_v[:, j], swa_sink[j]
            else:
                w, wo, qn, kn = qkn_wqkv[j], qkn_wo[j], qkn_qnorm[j], qkn_knorm[j]
                ck, cv, sink = cache_qkn_k[:, j], cache_qkn_v[:, j], None
            w = w.astype(BF16)
            qc, kc, vc, k32, v32 = _proj_gqa(xc, m, norm_mix[i], w, qn, kn, None, latent=False, tm=tm_c)
            ql, kl, vl = _proj_gqa(xl, m, norm_mix[i], w, qn, kn, gqa_tables, latent=True, tm=tm_l)
            wide = N_HEADS * HEAD_DIM
            oc = _attention(qc.reshape(nb, sc, wide), [(kc.reshape(nb, sc, wide), vc.reshape(nb, sc, wide))],
                            gqa=True, steps=1, tq=sc, sink=sink)
            ol = _attention(ql.reshape(nd, sl, wide),
                            [(kl.reshape(nd, sl, wide), vl.reshape(nd, sl, wide)),
                             (_cache_head_layout(ck, 0.0, split=True), _cache_head_layout(cv, 1.0))],
                            gqa=True, steps=1, tq=ATTN_Q_BAND if kind == 0 else ATTN_Q,
                            band=(kind == 0), sink=sink)
            names = ("swa_k", "swa_v") if kind == 0 else ("qkn_k", "qkn_v")
            states[names[0]].append(k32.reshape(nb, sc, N_KV_HEADS, HEAD_DIM))
            states[names[1]].append(v32.reshape(nb, sc, N_KV_HEADS, HEAD_DIM))
        else:
            wd, wuq, wk, wv = _mla_weights(mla_wdq[j], mla_wdkv[j], mla_wuq[j], mla_wukv[j])
            wo = mla_wo[j]
            qc, ckc, ckv32, kpe32 = _proj_mla(xc, m, norm_mix[i], wd, mla_qnorm[j], wuq, mla_kvnorm[j],
                                              None, None, latent=False, tm=tm_c)
            ql, ckl = _proj_mla(xl, m, norm_mix[i], wd, mla_qnorm[j], wuq, mla_kvnorm[j],
                                mla_q_tables, mla_k_tables, latent=True, tm=tm_c)
            cache = jnp.concatenate(
                [cache_mla_ckv[:, j], cache_mla_kpe[:, j], jnp.ones((nd, past, 1), F32),
                 jnp.zeros((nd, past, MLA_CK - MLA_KV_LORA - MLA_ROPE - 1), F32)], axis=-1).astype(BF16)
            ck_all = jnp.concatenate([ckl.reshape(nd, sl, MLA_CK), cache], axis=1)
            wq = MLA_HEADS * LANES
            kc, vc = _kv_expand(ckc, wk, wv, tm_c)
            kl, vl = _kv_expand(ck_all.reshape(nd * (sl + past), MLA_CK), wk, wv, 512)
            oc = _attention(qc.reshape(nb, sc, wq), [(kc.reshape(nb, sc, wq), vc.reshape(nb, sc, wq))],
                            gqa=False, steps=1, tq=sc)
            ol = _attention(ql.reshape(nd, sl, wq), [(kl.reshape(nd, sl + past, wq), vl.reshape(nd, sl + past, wq))],
                            gqa=False, steps=MLA_HEADS // 8, tq=ATTN_Q_MLA)
            states["ckv"].append(ckv32.reshape(nb, sc, MLA_KV_LORA))
            states["kpe"].append(kpe32.reshape(nb, sc, MLA_ROPE))
        final = i == DEPTH - 1
        pre = (m, norm_ffn[i], wo.astype(BF16), wr_hi, wr_lo, rbias)
        post = (m, norm_ffn[i], wgu, wdn, final_norm)
        xc, slots, wts, meta = _moe_pre(xc, oc.reshape(nb * sc, D_MODEL), *pre, latent=False)
        xc = _moe(xc, slots, wts, meta, *post, layer=i, latent=False, final=final)
        xl, slots, wts, meta = _moe_pre(xl, ol.reshape(nd * sl, D_MODEL), *pre, latent=True)
        xl = _moe(xl, slots, wts, meta, *post, layer=i, latent=True, final=final)
    return (xc.reshape(nb, sc, D_MODEL), xl.reshape(nd, sl, D_MODEL),
            jnp.stack(states["swa_k"], axis=1), jnp.stack(states["swa_v"], axis=1),
            jnp.stack(states["qkn_k"], axis=1), jnp.stack(states["qkn_v"], axis=1),
            jnp.stack(states["ckv"], axis=1), jnp.stack(states["kpe"], axis=1))
```

```python
import functools

import numpy as np
import jax
import jax.numpy as jnp
from jax import lax
from jax.experimental import pallas as pl
from jax.experimental.pallas import tpu as pltpu

F32 = jnp.float32
BF16 = jnp.bfloat16

D_MODEL = 1024
DEPTH = 4
GRID_W = 64
N_MIXERS = 3
N_HEADS = 16
N_KV_HEADS = 4
HEAD_DIM = 64
QKV_DIM = (N_HEADS + 2 * N_KV_HEADS) * HEAD_DIM
WINDOW = 128
ROPE_THETA = 10000.0
MLA_HEADS = 16
MLA_Q_LORA = 384
MLA_KV_LORA = 256
MLA_NOPE = 64
MLA_ROPE = 32
MLA_V = 64
N_EXPERTS = 16
N_GROUPS = 4
EXPERTS_PER_GROUP = N_EXPERTS // N_GROUPS
D_EXPERT = 256
EPS = 1e-6

LANES = 128
COND_ROWS = 16
MLA_DOWN = 768
MLA_CK = 384
PROJ_TOKENS = 512
ATTN_Q = 256
ATTN_Q_MLA = 512
ATTN_Q_BAND = 256
MOE_BLOCK = 1024
MOE_ROWS = 192
MOE_STEP_EXPERTS = 2
SLOT_BITS = 16
MOE_UNROLL = 32
SPLIT_LANES = np.concatenate([np.arange(0, 32), np.arange(64, 96), np.arange(32, 64), np.arange(96, 128)])
NEG_BIG = -1e30
MXU_DENOM_MIN_KEYS = 1024
LOG2_E = 1.4426950408889634
VMEM_LIMIT = 56 * 1024 * 1024


def _cparams(sem):
    return pltpu.CompilerParams(dimension_semantics=sem, vmem_limit_bytes=VMEM_LIMIT)


def _sigmoid(x):
    return 1.0 / (1.0 + jnp.exp(-x))


def _rms(x, g):
    return x * lax.rsqrt(jnp.mean(x * x, axis=-1, keepdims=True) + EPS) * g


def _modulate(x, g, shift, scale):
    return x * lax.rsqrt(jnp.mean(x * x, axis=-1, keepdims=True) + EPS) * (g * (1.0 + scale)) + shift


def _dot(a, b):
    return jnp.dot(a, b, preferred_element_type=F32)


def _dot_nt(a, b):
    return lax.dot_general(a, b, (((1,), (1,)), ((), ())), preferred_element_type=F32)


def _adaln_kernel(cond_ref, w_ref, b_ref, o_ref):
    c = cond_ref[...]
    s = (c * _sigmoid(c)).astype(BF16)
    o_ref[...] = _dot(s, w_ref[...].astype(BF16)) + b_ref[...]


def _adaln(cond, w_mod, b_mod):
    tn = 1536
    n = 6 * D_MODEL
    out = pl.pallas_call(
        _adaln_kernel,
        grid=(DEPTH, n // tn),
        in_specs=[
            pl.BlockSpec((COND_ROWS, D_MODEL), lambda l, j: (0, 0)),
            pl.BlockSpec((None, D_MODEL, tn), lambda l, j: (l, 0, j)),
            pl.BlockSpec((None, 1, tn), lambda l, j: (l, 0, j)),
        ],
        out_specs=pl.BlockSpec((None, COND_ROWS, tn), lambda l, j: (l, 0, j)),
        out_shape=jax.ShapeDtypeStruct((DEPTH, COND_ROWS, n), F32),
        compiler_params=_cparams(("parallel", "parallel")),
        name="adaln",
    )(cond, w_mod, b_mod.reshape(DEPTH, 1, n))
    return out.reshape(DEPTH, COND_ROWS, 6, D_MODEL)


def _mod_spec(rows_per_cond):
    if rows_per_cond == 0:
        return pl.BlockSpec((1, 6, D_MODEL), lambda i: (0, 0, 0))
    return pl.BlockSpec((1, 6, D_MODEL), lambda i: (1 + i // rows_per_cond, 0, 0))


def _rope_tables(seq, head_dim, lane0, period):
    half, quarter = head_dim // 2, head_dim // 4
    pos = np.arange(seq)
    row, col = (pos // GRID_W).astype(np.float64), (pos % GRID_W).astype(np.float64)
    inv = ROPE_THETA ** (-np.arange(quarter, dtype=np.float64) / quarter)
    ang = np.concatenate([row[:, None] * inv, col[:, None] * inv], axis=-1)
    cos = np.ones((seq, LANES))
    s1 = np.zeros((seq, LANES))
    s2 = np.zeros((seq, LANES))
    starts = [lane0] if period == 0 else list(range(lane0, LANES, period))
    for st in starts:
        cos[:, st:st + half] = np.cos(ang)
        cos[:, st + half:st + head_dim] = np.cos(ang)
        s1[:, st:st + half] = -np.sin(ang)
        s2[:, st + half:st + head_dim] = np.sin(ang)
    return cos, s1, s2


def _rope_chunk(x, cos, s1, s2, half):
    return x * cos + pltpu.roll(x, LANES - half, 1) * s1 + pltpu.roll(x, half, 1) * s2


def _head_layout(kv, fill, split=False):
    lane = lax.broadcasted_iota(jnp.int32, (kv.shape[0], LANES), 1)
    half = HEAD_DIM // 2 if split else HEAD_DIM
    lo = (lane // half) % 2 == 0
    zero = jnp.full((kv.shape[0], LANES), fill, F32)
    out = []
    for j in range(2):
        ch = kv[:, LANES * j:LANES * (j + 1)]
        up, down = pltpu.roll(ch, half, 1), pltpu.roll(ch, LANES - half, 1)
        out += [jnp.where(lo, ch, zero), jnp.where(lo, zero, up), jnp.where(lo, down, zero), jnp.where(lo, zero, ch)]
    return jnp.concatenate(out, axis=1)


def _proj_gqa_kernel(*refs, qknorm, rope, state):
    it = iter(refs)
    x_ref, mod_ref, g_ref, w_ref = next(it), next(it), next(it), next(it)
    if qknorm:
        qn_ref, kn_ref, gm_ref = next(it), next(it), next(it)
    if rope:
        cos_ref, sin_ref = next(it), next(it)
    q_ref, kp_ref, vp_ref = next(it), next(it), next(it)
    if state:
        k32_ref, v32_ref = next(it), next(it)

    mod = mod_ref[0]
    h = _modulate(x_ref[...], g_ref[...], mod[0:1], mod[1:2])
    qkv = _dot(h.astype(BF16), w_ref[...])
    nq = N_HEADS * HEAD_DIM
    nk = N_KV_HEADS * HEAD_DIM
    chunks = [qkv[:, LANES * j:LANES * (j + 1)] for j in range((nq + nk) // LANES)]
    if qknorm:
        gains = [qn_ref[...]] * (nq // LANES) + [kn_ref[...]] * (nk // LANES)
        gm = gm_ref[...]
        chunks = [c * lax.rsqrt(_dot((c * c).astype(BF16), gm) + EPS) * g for c, g in zip(chunks, gains)]
    k_state = jnp.concatenate(chunks[nq // LANES:], axis=1)
    if rope:
        cos, sin = cos_ref[...], sin_ref[...]
        chunks = [c * cos + pltpu.roll(c, LANES // 2, 1) * sin for c in chunks]
    q = jnp.concatenate(chunks[:nq // LANES], axis=1) * (HEAD_DIM ** -0.5 * LOG2_E)
    k = jnp.concatenate(chunks[nq // LANES:], axis=1)
    v = qkv[:, nq + nk:]
    q_ref[...] = q.astype(BF16)
    kp_ref[...] = _head_layout(k, 0.0, split=rope).astype(BF16)
    vp_ref[...] = _head_layout(v, 1.0).astype(BF16)
    if state:
        k32_ref[...] = k_state
        v32_ref[...] = v


def _proj_gqa(x, mods, gain, w, qn, kn, tables, *, latent, tm):
    n = x.shape[0]
    qknorm = qn is not None
    tiles_per_seq = 2048 // tm
    order = SPLIT_LANES if latent else np.arange(LANES)
    if latent:
        nqk = (N_HEADS + N_KV_HEADS) * HEAD_DIM
        half = HEAD_DIM // 2
        qk = w[:, :nqk].reshape(D_MODEL, nqk // LANES, 2, 2, half).swapaxes(2, 3).reshape(D_MODEL, nqk)
        w = jnp.concatenate([qk, w[:, nqk:]], axis=1)
    ins = [x, mods, gain.reshape(1, D_MODEL), w]
    specs = [
        pl.BlockSpec((tm, D_MODEL), lambda i: (i, 0)),
        _mod_spec(tiles_per_seq if latent else 0),
        pl.BlockSpec((1, D_MODEL), lambda i: (0, 0)),
        pl.BlockSpec((D_MODEL, QKV_DIM), lambda i: (0, 0)),
    ]
    if qknorm:
        head = order // HEAD_DIM
        gm = (head[:, None] == head[None, :]) / HEAD_DIM
        lanes = lambda g: jnp.tile(g, 2).reshape(2, 2, -1).swapaxes(0, 1).reshape(1, LANES) if latent else jnp.tile(g, 2).reshape(1, LANES)
        ins += [lanes(qn), lanes(kn), jnp.asarray(gm, BF16)]
        specs += [pl.BlockSpec((1, LANES), lambda i: (0, 0))] * 2 + [pl.BlockSpec((LANES, LANES), lambda i: (0, 0))]
    if latent:
        ins += list(tables)
        specs += [pl.BlockSpec((tm, LANES), lambda i: (i % tiles_per_seq, 0))] * 2
    wide = N_HEADS * HEAD_DIM
    outs = [jax.ShapeDtypeStruct((n, wide), BF16)] * 3
    ospecs = [pl.BlockSpec((tm, wide), lambda i: (i, 0))] * 3
    if not latent:
        nk = N_KV_HEADS * HEAD_DIM
        outs += [jax.ShapeDtypeStruct((n, nk), F32)] * 2
        ospecs += [pl.BlockSpec((tm, nk), lambda i: (i, 0))] * 2
    return pl.pallas_call(
        functools.partial(_proj_gqa_kernel, qknorm=qknorm, rope=latent, state=not latent),
        grid=(n // tm,),
        in_specs=specs,
        out_specs=ospecs,
        out_shape=outs,
        compiler_params=_cparams(("parallel",)),
        name="proj_gqa_lat" if latent else "proj_gqa_ctx",
    )(*ins)


def _proj_mla_kernel(*refs, rope, state):
    it = iter(refs)
    x_ref, mod_ref, g_ref, wd_ref, qn_ref, wuq_ref, kvn_ref = (next(it) for _ in range(7))
    if rope:
        qt = [next(it) for _ in range(3)]
        kt = [next(it) for _ in range(3)]
    q_ref, ck_ref = next(it), next(it)
    if state:
        ckv_ref, kpe_ref = next(it), next(it)

    mod = mod_ref[0]
    h = _modulate(x_ref[...], g_ref[...], mod[0:1], mod[1:2])
    d = _dot(h.astype(BF16), wd_ref[...])
    cq = _rms(d[:, :MLA_Q_LORA], qn_ref[...])
    q = _dot(cq.astype(BF16), wuq_ref[...])
    ckv = _rms(d[:, MLA_Q_LORA:MLA_Q_LORA + MLA_KV_LORA], kvn_ref[...])
    kpe = d[:, MLA_Q_LORA + MLA_KV_LORA:]
    kpe_state = kpe
    scale = (MLA_NOPE + MLA_ROPE) ** -0.5 * LOG2_E
    qs = [q[:, LANES * j:LANES * (j + 1)] for j in range(MLA_HEADS)]
    if rope:
        cos, s1, s2 = (t[...] for t in qt)
        qs = [_rope_chunk(c, cos, s1, s2, MLA_ROPE // 2) for c in qs]
        kpe = _rope_chunk(kpe, *(t[...] for t in kt), MLA_ROPE // 2)
    q_ref[...] = (jnp.concatenate(qs, axis=1) * scale).astype(BF16)
    ones = lax.broadcasted_iota(jnp.int32, kpe.shape, 1) == MLA_ROPE
    ck_ref[...] = jnp.concatenate([ckv, jnp.where(ones, 1.0, kpe)], axis=1).astype(BF16)
    if state:
        ckv_ref[...] = ckv
        kpe_ref[...] = kpe_state[:, :MLA_ROPE]


def _proj_mla(x, mods, gain, wd, qn, wuq, kvn, qtables, ktables, *, latent, tm):
    n = x.shape[0]
    tiles_per_seq = 2048 // tm
    wq = MLA_HEADS * LANES
    ins = [x, mods, gain.reshape(1, D_MODEL), wd, qn.reshape(1, MLA_Q_LORA), wuq, kvn.reshape(1, MLA_KV_LORA)]
    specs = [
        pl.BlockSpec((tm, D_MODEL), lambda i: (i, 0)),
        _mod_spec(tiles_per_seq if latent else 0),
        pl.BlockSpec((1, D_MODEL), lambda i: (0, 0)),
        pl.BlockSpec((D_MODEL, MLA_DOWN), lambda i: (0, 0)),
        pl.BlockSpec((1, MLA_Q_LORA), lambda i: (0, 0)),
        pl.BlockSpec((MLA_Q_LORA, wq), lambda i: (0, 0)),
        pl.BlockSpec((1, MLA_KV_LORA), lambda i: (0, 0)),
    ]
    if latent:
        ins += list(qtables) + list(ktables)
        specs += [pl.BlockSpec((tm, LANES), lambda i: (i % tiles_per_seq, 0))] * 6
    outs = [jax.ShapeDtypeStruct((n, wq), BF16), jax.ShapeDtypeStruct((n, MLA_CK), BF16)]
    ospecs = [pl.BlockSpec((tm, wq), lambda i: (i, 0)), pl.BlockSpec((tm, MLA_CK), lambda i: (i, 0))]
    if not latent:
        outs += [jax.ShapeDtypeStruct((n, MLA_KV_LORA), F32), jax.ShapeDtypeStruct((n, MLA_ROPE), F32)]
        ospecs += [pl.BlockSpec((tm, MLA_KV_LORA), lambda i: (i, 0)), pl.BlockSpec((tm, MLA_ROPE), lambda i: (i, 0))]
    return pl.pallas_call(
        functools.partial(_proj_mla_kernel, rope=latent, state=not latent),
        grid=(n // tm,),
        in_specs=specs,
        out_specs=ospecs,
        out_shape=outs,
        compiler_params=_cparams(("parallel",)),
        name="proj_mla_lat" if latent else "proj_mla_ctx",
    )(*ins)


def _kv_expand_kernel(ck_ref, wk_ref, wv_ref, k_ref, v_ref):
    ck = ck_ref[...]
    k_ref[...] = _dot(ck, wk_ref[...]).astype(BF16)
    v_ref[...] = _dot(ck, wv_ref[...]).astype(BF16)


def _kv_expand(ck, wk, wv, tm):
    n = ck.shape[0]
    wide = MLA_HEADS * LANES
    return pl.pallas_call(
        _kv_expand_kernel,
        grid=(n // tm,),
        in_specs=[
            pl.BlockSpec((tm, MLA_CK), lambda i: (i, 0)),
            pl.BlockSpec((MLA_CK, wide), lambda i: (0, 0)),
            pl.BlockSpec((MLA_CK, wide), lambda i: (0, 0)),
        ],
        out_specs=[pl.BlockSpec((tm, wide), lambda i: (i, 0))] * 2,
        out_shape=[jax.ShapeDtypeStruct((n, wide), BF16)] * 2,
        compiler_params=_cparams(("parallel",)),
        name="mla_kv_expand",
    )(ck, wk, wv)


def _attn_kernel(*refs, units, n_seg, tq, band, use_sink, heads_per_step, mxu_denom):
    it = iter(refs)
    if use_sink:
        sink_ref = next(it)
    if band:
        bias_ref = next(it)
    q_ref = next(it)
    kv_refs = [(next(it), next(it)) for _ in range(n_seg)]
    o_ref = next(it)

    if band:
        qi = pl.program_id(2)
        seq = kv_refs[0][0].shape[0]
        win = tq + 2 * WINDOW
        start = pl.multiple_of(jnp.clip(qi * tq - WINDOW, 0, seq - win), WINDOW)

    for members, ke, ko in units:
        rows = tq * len(members)
        low_half = lax.broadcasted_iota(jnp.int32, (rows, LANES), 1) < HEAD_DIM
        halves = []
        for parity, koff in enumerate((ke, ko)):
            qh = jnp.concatenate([q_ref[:, mem[parity]:mem[parity] + LANES] for mem in members], axis=0)
            scores, values = [], []
            for si, (k_ref, v_ref) in enumerate(kv_refs):
                if band and si == 0:
                    kh = k_ref[pl.ds(start, win), koff:koff + LANES]
                    vh = v_ref[pl.ds(start, win), koff:koff + LANES]
                    s = _dot_nt(qh, kh) + jnp.concatenate([bias_ref[...]] * len(members), axis=0)
                else:
                    kh = k_ref[:, koff:koff + LANES]
                    vh = v_ref[:, koff:koff + LANES]
                    s = _dot_nt(qh, kh)
                scores.append(s)
                values.append(vh)
            m = functools.reduce(jnp.maximum, [jnp.max(s, axis=-1, keepdims=True) for s in scores])
            if use_sink:
                head0 = pl.program_id(1) * heads_per_step + parity
                sk = sink_ref[head0 + members[-1][3]] * LOG2_E
                row = lax.broadcasted_iota(jnp.int32, (rows, 1), 0)
                for i, mem in reversed(list(enumerate(members[:-1]))):
                    sk = jnp.where(row < tq * (i + 1), sink_ref[head0 + mem[3]] * LOG2_E, sk)
                m = jnp.maximum(m, sk)
            es = [jnp.exp2(s - m) for s in scores]
            pv = functools.reduce(jnp.add, [_dot(e.astype(BF16), vh) for e, vh in zip(es, values)])
            if mxu_denom:
                denom = pltpu.roll(pv, HEAD_DIM, 1)
            else:
                denom = functools.reduce(jnp.add, [jnp.sum(e, axis=-1, keepdims=True) for e in es])
            if use_sink:
                denom = denom + jnp.exp2(sk - m)
            halves.append(pv / denom)
        out = jnp.where(low_half, halves[0], halves[1]).astype(o_ref.dtype)
        for i, mem in enumerate(members):
            o_ref[:, mem[2]:mem[2] + LANES] = out[tq * i:tq * (i + 1)]


def _attention(q, segs, *, gqa, steps, tq, band=False, sink=None):
    b, sq, wq_total = q.shape
    wk_total = segs[0][0].shape[2]
    wo_total = N_HEADS * HEAD_DIM
    wq, wk, wo = wq_total // steps, wk_total // steps, wo_total // steps
    n_pairs = wo // LANES
    if gqa:
        pair = lambda p: (LANES * p, LANES * p, LANES * p, 2 * p)
        units = [([pair(2 * g), pair(2 * g + 1)], 2 * LANES * g, 2 * LANES * g + LANES) for g in range(n_pairs // 2)]
    else:
        pair = lambda p: (2 * LANES * p, 2 * LANES * p + LANES, LANES * p, 2 * p)
        units = [([pair(p)], 2 * LANES * p, 2 * LANES * p + LANES) for p in range(n_pairs)]
    ins, specs = [], []
    if sink is not None:
        ins.append(sink)
        specs.append(pl.BlockSpec(memory_space=pltpu.SMEM))
    if band:
        win, n_q = tq + 2 * WINDOW, sq // tq
        rel = np.arange(win)[None, :] - np.arange(tq)[:, None]
        bias = np.stack([np.where(np.abs(rel - off) <= WINDOW, 0.0, NEG_BIG) for off in (0, WINDOW, 2 * WINDOW)])
        ins.append(jnp.asarray(bias, F32))
        specs.append(pl.BlockSpec(
            (None, tq, win), lambda bi, si, qi: (jnp.where(qi == 0, 0, jnp.where(qi == n_q - 1, 2, 1)), 0, 0)))
    ins.append(q)
    specs.append(pl.BlockSpec((None, tq, wq), lambda bi, si, qi: (bi, qi, si)))
    for k, v in segs:
        sk = k.shape[1]
        ins += [k, v]
        specs += [pl.BlockSpec((None, sk, wk), lambda bi, si, qi: (bi, 0, si))] * 2
    return pl.pallas_call(
        functools.partial(_attn_kernel, units=units, n_seg=len(segs), tq=tq, band=band,
                          use_sink=sink is not None, heads_per_step=2 * n_pairs,
                          mxu_denom=sum(k.shape[1] for k, _ in segs) >= MXU_DENOM_MIN_KEYS),
        grid=(b, steps, sq // tq),
        in_specs=specs,
        out_specs=pl.BlockSpec((None, tq, wo), lambda bi, si, qi: (bi, qi, si)),
        out_shape=jax.ShapeDtypeStruct((b, sq, wo_total), BF16),
        compiler_params=_cparams(("parallel", "parallel", "parallel")),
        name="attention",
    )(*ins)


def _route(logits_t, bias_col):
    s = _sigmoid(logits_t)
    sb = s + bias_col
    r = [sb[e:e + 1, :] for e in range(N_EXPERTS)]
    sr = [s[e:e + 1, :] for e in range(N_EXPERTS)]
    gscore = []
    for g in range(N_GROUPS):
        a = r[EXPERTS_PER_GROUP * g:EXPERTS_PER_GROUP * (g + 1)]
        best = None
        for i in range(EXPERTS_PER_GROUP):
            for j in range(i + 1, EXPERTS_PER_GROUP):
                p = a[i] + a[j]
                best = p if best is None else jnp.maximum(best, p)
        gscore.append(best)
    gbest, gsel = gscore[0], jnp.zeros_like(gscore[0], dtype=jnp.int32)
    for g in range(1, N_GROUPS):
        better = gscore[g] > gbest
        gsel = jnp.where(better, g, gsel)
        gbest = jnp.where(better, gscore[g], gbest)
    neg = jnp.full_like(r[0], -jnp.inf)
    cand = [jnp.where(gsel == e // EXPERTS_PER_GROUP, r[e], neg) for e in range(N_EXPERTS)]

    def first_argmax(vals):
        best, idx, w = vals[0], jnp.zeros_like(gsel), sr[0]
        for e in range(1, N_EXPERTS):
            better = vals[e] > best
            idx = jnp.where(better, e, idx)
            w = jnp.where(better, sr[e], w)
            best = jnp.where(better, vals[e], best)
        return idx, w

    i1, w1 = first_argmax(cand)
    i2, w2 = first_argmax([jnp.where(i1 == e, neg, cand[e]) for e in range(N_EXPERTS)])
    tot = w1 + w2
    return i1, i2, w1 / tot, w2 / tot


def _moe_pre_kernel(x_ref, o_ref, mod_ref, g_ref, wo_ref, wr_hi_ref, wr_lo_ref, rb_ref, tri_ref,
                    xnew_ref, slot_ref, wts_ref, meta_ref):
    mod = mod_ref[0]
    xn = x_ref[...] + mod[2:3] * _dot(o_ref[...], wo_ref[...])
    xnew_ref[...] = xn
    h = _modulate(xn, g_ref[...], mod[3:4], mod[4:5])
    h_hi = h.astype(BF16)
    h_lo = (h - h_hi.astype(F32)).astype(BF16)
    logits_t = _dot_nt(wr_hi_ref[...], h_hi) + _dot_nt(wr_hi_ref[...], h_lo) + _dot_nt(wr_lo_ref[...], h_hi)
    i1, i2, g1, g2 = _route(logits_t, rb_ref[...])
    pad = jnp.zeros((LANES - 2, g1.shape[1]), F32)
    wts_ref[...] = jnp.concatenate([g1, g2, pad], axis=0).T

    rows = lax.broadcasted_iota(jnp.int32, logits_t.shape, 0)
    member = jnp.where((rows == i1) | (rows == i2), 1.0, 0.0)
    rank = _dot(member.astype(BF16), tri_ref[...])
    count = jnp.sum(member, axis=1, keepdims=True)
    lane = lax.broadcasted_iota(jnp.int32, (1, LANES), 1)
    start = jnp.zeros((1, 1), F32)
    slot1 = jnp.zeros_like(g1)
    slot2 = jnp.zeros_like(g1)
    starts = jnp.zeros((1, LANES), F32)
    chunks = jnp.zeros((1, LANES), F32)
    for e in range(N_EXPERTS):
        pos = start + rank[e:e + 1, :]
        slot1 = jnp.where(i1 == e, pos, slot1)
        slot2 = jnp.where(i2 == e, pos, slot2)
        cnt = count[e:e + 1, :]
        starts = jnp.where(lane == e, start, starts)
        chunks = jnp.where(lane == e, jnp.floor((cnt + (MOE_ROWS - 1)) * (1.0 / MOE_ROWS)), chunks)
        start = start + cnt
    sub = D_MODEL // LANES
    slot_ref[...] = slot1.astype(jnp.int32) * sub + slot2.astype(jnp.int32) * (sub << SLOT_BITS)
    meta_ref[...] = jnp.concatenate([starts, chunks], axis=0).astype(jnp.int32)


def _moe_pre(x, o, mods, gain, wo, wr_hi, wr_lo, rbias, *, latent):
    n = x.shape[0]
    tm = MOE_BLOCK
    const = lambda i: (0, 0)
    tri = jnp.asarray(np.triu(np.ones((tm, tm), np.float32), k=1), BF16)
    return pl.pallas_call(
        _moe_pre_kernel,
        grid=(n // tm,),
        in_specs=[
            pl.BlockSpec((tm, D_MODEL), lambda i: (i, 0)),
            pl.BlockSpec((tm, D_MODEL), lambda i: (i, 0)),
            _mod_spec((2048 // tm) if latent else 0),
            pl.BlockSpec((1, D_MODEL), const),
            pl.BlockSpec((D_MODEL, D_MODEL), const),
            pl.BlockSpec((LANES, D_MODEL), const),
            pl.BlockSpec((LANES, D_MODEL), const),
            pl.BlockSpec((LANES, 1), const),
            pl.BlockSpec((tm, tm), const),
        ],
        out_specs=[
            pl.BlockSpec((tm, D_MODEL), lambda i: (i, 0)),
            pl.BlockSpec((None, 1, tm), lambda i: (i, 0, 0)),
            pl.BlockSpec((tm, LANES), lambda i: (i, 0)),
            pl.BlockSpec((None, 2, LANES), lambda i: (i, 0, 0)),
        ],
        out_shape=[
            jax.ShapeDtypeStruct((n, D_MODEL), F32),
            jax.ShapeDtypeStruct((n // tm, 1, tm), jnp.int32),
            jax.ShapeDtypeStruct((n, LANES), F32),
            jax.ShapeDtypeStruct((n // tm, 2, LANES), jnp.int32),
        ],
        compiler_params=_cparams(("parallel",)),
        name="moe_pre_lat" if latent else "moe_pre_ctx",
    )(x, o, mods, gain.reshape(1, D_MODEL), wo, wr_hi, wr_lo, rbias, tri)


def _moe_kernel(slot_ref, start_ref, chunks_ref,
                xnew_ref, wts_ref, mod_ref, g_ref, wgu_ref, wd_ref, fin_ref, out_ref,
                tok_ref, tok2_ref, xs_ref, y_ref, *, tb, final):
    mask = (1 << SLOT_BITS) - 1
    b, e = pl.program_id(0), pl.program_id(1)
    tok0 = b * tb
    sub = D_MODEL // LANES

    @pl.when(e == 0)
    def _():
        @pl.when(b == 0)
        def _():
            xs_ref[2 * tb * sub:, :] = jnp.zeros((MOE_ROWS * sub, LANES), F32)

        mod = mod_ref[0]
        h = _modulate(xnew_ref[...], g_ref[...], mod[3:4], mod[4:5])
        for s in range(sub):
            tok_ref[pl.ds(s, tb, stride=sub), :] = h[:, LANES * s:LANES * (s + 1)]

        def dispatch(i, carry):
            for k in range(MOE_UNROLL):
                t = i * MOE_UNROLL + k
                row = tok_ref[pl.ds(pl.multiple_of(t * sub, sub), sub), :]
                packed = slot_ref[tok0 + t]
                xs_ref[pl.ds(pl.multiple_of(packed & mask, sub), sub), :] = row
                xs_ref[pl.ds(pl.multiple_of(packed >> SLOT_BITS, sub), sub), :] = row
            return carry
        lax.fori_loop(0, tb // MOE_UNROLL, dispatch, 0)

    def chunk(k, c):
        meta = b * N_EXPERTS + e * MOE_STEP_EXPERTS + k
        base = pl.multiple_of((start_ref[meta] + c * MOE_ROWS) * sub, sub)
        rows_in = xs_ref.at[pl.ds(base, MOE_ROWS * sub)]
        lhs = jnp.concatenate([rows_in[pl.ds(s, MOE_ROWS, stride=sub), :] for s in range(sub)], axis=1)
        au = _dot(lhs.astype(BF16), wgu_ref[k])
        a, u = au[:, :D_EXPERT], au[:, D_EXPERT:]
        y = _dot((a * _sigmoid(a) * u).astype(BF16), wd_ref[k])
        rows_out = y_ref.at[pl.ds(base, MOE_ROWS * sub)]
        for s in range(sub):
            rows_out[pl.ds(s, MOE_ROWS, stride=sub), :] = y[:, LANES * s:LANES * (s + 1)]

    n_chunks = [chunks_ref[b * N_EXPERTS + e * MOE_STEP_EXPERTS + k] for k in range(MOE_STEP_EXPERTS)]
    single = functools.reduce(jnp.logical_and, [n <= 1 for n in n_chunks])

    @pl.when(single)
    def _():
        for k in range(MOE_STEP_EXPERTS):
            chunk(k, 0)

    @pl.when(jnp.logical_not(single))
    def _():
        for k in range(MOE_STEP_EXPERTS):
            def body(c, carry, k=k):
                chunk(k, c)
                return carry
            lax.fori_loop(0, n_chunks[k], body, 0)

    @pl.when(e == N_EXPERTS // MOE_STEP_EXPERTS - 1)
    def _():
        def combine(i, carry):
            for k in range(MOE_UNROLL):
                t = i * MOE_UNROLL + k
                packed = slot_ref[tok0 + t]
                dst = pl.ds(pl.multiple_of(t * sub, sub), sub)
                tok_ref[dst, :] = y_ref[pl.ds(pl.multiple_of(packed & mask, sub), sub), :]
                tok2_ref[dst, :] = y_ref[pl.ds(pl.multiple_of(packed >> SLOT_BITS, sub), sub), :]
            return carry
        lax.fori_loop(0, tb // MOE_UNROLL, combine, 0)
        first = jnp.concatenate([tok_ref[pl.ds(s, tb, stride=sub), :] for s in range(sub)], axis=1)
        second = jnp.concatenate([tok2_ref[pl.ds(s, tb, stride=sub), :] for s in range(sub)], axis=1)
        wts = wts_ref[...]
        moe = wts[:, 0:1] * first + wts[:, 1:2] * second
        out = xnew_ref[...] + mod_ref[0][5:6] * moe
        if final:
            out = _rms(out, fin_ref[...])
        out_ref[...] = out


def _moe(xnew, slots, wts, meta, mods, gain, wgu, wd, fin, *, layer, latent, final):
    n = xnew.shape[0]
    tb = MOE_BLOCK
    cap = 2 * tb + MOE_ROWS
    sub = D_MODEL // LANES
    assert cap * sub < (1 << SLOT_BITS)
    const = lambda i, e: (0, 0)
    rows_per_cond = (2048 // tb) if latent else 0
    if rows_per_cond == 0:
        mod_spec = pl.BlockSpec((1, 6, D_MODEL), lambda i, e: (0, 0, 0))
    else:
        mod_spec = pl.BlockSpec((1, 6, D_MODEL), lambda i, e: (1 + i // rows_per_cond, 0, 0))
    smem = pl.BlockSpec(memory_space=pltpu.SMEM)
    return pl.pallas_call(
        functools.partial(_moe_kernel, tb=tb, final=final),
        grid=(n // tb, N_EXPERTS // MOE_STEP_EXPERTS),
        in_specs=[smem] * 3 + [
            pl.BlockSpec((tb, D_MODEL), lambda i, e: (i, 0)),
            pl.BlockSpec((tb, LANES), lambda i, e: (i, 0)),
            mod_spec,
            pl.BlockSpec((1, D_MODEL), const),
            pl.BlockSpec((None, MOE_STEP_EXPERTS, D_MODEL, 2 * D_EXPERT), lambda i, e: (layer, e, 0, 0)),
            pl.BlockSpec((None, MOE_STEP_EXPERTS, D_EXPERT, D_MODEL), lambda i, e: (layer, e, 0, 0)),
            pl.BlockSpec((1, D_MODEL), const),
        ],
        out_specs=pl.BlockSpec((tb, D_MODEL), lambda i, e: (i, 0)),
        out_shape=jax.ShapeDtypeStruct((n, D_MODEL), F32),
        scratch_shapes=[
            pltpu.VMEM((tb * sub, LANES), F32),
            pltpu.VMEM((tb * sub, LANES), F32),
            pltpu.VMEM((cap * sub, LANES), F32),
            pltpu.VMEM((cap * sub, LANES), F32),
        ],
        compiler_params=_cparams(("arbitrary", "arbitrary")),
        name="moe_lat" if latent else "moe_ctx",
    )(slots.reshape(n), meta[:, 0, :N_EXPERTS].reshape(-1), meta[:, 1, :N_EXPERTS].reshape(-1),
      xnew, wts, mods, gain.reshape(1, D_MODEL), wgu, wd, fin.reshape(1, D_MODEL))


def _expert_weights_kernel(wg_ref, wu_ref, wd_ref, wgu_ref, wdn_ref):
    wgu_ref[:, :D_EXPERT] = wg_ref[...].astype(BF16)
    wgu_ref[:, D_EXPERT:] = wu_ref[...].astype(BF16)
    wdn_ref[...] = wd_ref[...].astype(BF16)


def _expert_weights(w_gate, w_up, w_down):
    up = pl.BlockSpec((None, None, D_MODEL, D_EXPERT), lambda l, e: (l, e, 0, 0))
    down = pl.BlockSpec((None, None, D_EXPERT, D_MODEL), lambda l, e: (l, e, 0, 0))
    return pl.pallas_call(
        _expert_weights_kernel,
        grid=(DEPTH, N_EXPERTS),
        in_specs=[up, up, down],
        out_specs=[pl.BlockSpec((None, None, D_MODEL, 2 * D_EXPERT), lambda l, e: (l, e, 0, 0)), down],
        out_shape=[jax.ShapeDtypeStruct((DEPTH, N_EXPERTS, D_MODEL, 2 * D_EXPERT), BF16),
                   jax.ShapeDtypeStruct((DEPTH, N_EXPERTS, D_EXPERT, D_MODEL), BF16)],
        compiler_params=_cparams(("parallel", "parallel")),
        name="expert_weights",
    )(w_gate, w_up, w_down)


def _cache_head_layout(c, fill, split=False):
    z = jnp.full_like(c, fill)
    even = jnp.concatenate([c, z], axis=-1)
    odd = jnp.concatenate([z, c], axis=-1)
    if split:
        split = lambda a: a.reshape(*a.shape[:-1], 2, 2, HEAD_DIM // 2).swapaxes(-2, -3).reshape(a.shape)
        even, odd = split(even), split(odd)
    return jnp.stack([even, odd], axis=3).reshape(c.shape[0], c.shape[1], 2 * N_KV_HEADS * LANES).astype(BF16)


def _mla_weights(wdq, wdkv, wuq, wukv):
    wd = jnp.concatenate([wdq, wdkv, jnp.zeros((D_MODEL, MLA_DOWN - MLA_Q_LORA - MLA_KV_LORA - MLA_ROPE), F32)], axis=1)
    dk = MLA_NOPE + MLA_ROPE
    wuq_p = jnp.pad(wuq.reshape(MLA_Q_LORA, MLA_HEADS, dk), ((0, 0), (0, 0), (0, LANES - dk)))
    wuq_p = wuq_p.reshape(MLA_Q_LORA, MLA_HEADS * LANES)
    kv = wukv.reshape(MLA_KV_LORA, MLA_HEADS, MLA_NOPE + MLA_V)
    k_nope, v = kv[..., :MLA_NOPE], kv[..., MLA_NOPE:]
    wk_top = jnp.pad(k_nope, ((0, 0), (0, 0), (0, LANES - MLA_NOPE))).reshape(MLA_KV_LORA, MLA_HEADS * LANES)
    place = np.zeros((MLA_CK - MLA_KV_LORA, MLA_HEADS, LANES), np.float32)
    for r in range(MLA_ROPE):
        place[r, :, MLA_NOPE + r] = 1.0
    wk = jnp.concatenate([wk_top, jnp.asarray(place.reshape(MLA_CK - MLA_KV_LORA, MLA_HEADS * LANES))], axis=0)
    z = jnp.zeros_like(v)
    v_even = jnp.concatenate([v, z], axis=-1)
    v_odd = jnp.concatenate([z, v], axis=-1)
    parity = (jnp.arange(MLA_HEADS) % 2 == 0)[None, :, None]
    wv_top = jnp.where(parity, v_even, v_odd).reshape(MLA_KV_LORA, MLA_HEADS * LANES)
    sums = np.zeros((MLA_CK - MLA_KV_LORA, MLA_HEADS, LANES), np.float32)
    sums[MLA_ROPE, 0::2, MLA_V:] = 1.0
    sums[MLA_ROPE, 1::2, :MLA_V] = 1.0
    wv = jnp.concatenate([wv_top, jnp.asarray(sums.reshape(MLA_CK - MLA_KV_LORA, MLA_HEADS * LANES))], axis=0)
    return wd.astype(BF16), wuq_p.astype(BF16), wk.astype(BF16), wv.astype(BF16)


def kernel(x_prompt, x_sample, cache_swa_k, cache_swa_v, cache_qkn_k, cache_qkn_v, cache_mla_ckv, cache_mla_kpe, c, c_ctx, w_mod, b_mod, norm_mix, norm_ffn, swa_wqkv, swa_wo, swa_sink, qkn_wqkv, qkn_wo, qkn_qnorm, qkn_knorm, mla_wdq, mla_qnorm, mla_wuq, mla_wdkv, mla_kvnorm, mla_wukv, mla_wo, w_router, router_bias, moe_w_gate, moe_w_up, moe_w_down, final_norm):
    nb, sc, _ = x_prompt.shape
    nd, sl, _ = x_sample.shape
    past = cache_swa_k.shape[2]
    cond = jnp.zeros((COND_ROWS, D_MODEL), F32).at[0].set(c_ctx).at[1:1 + nd].set(c)
    mods = _adaln(cond, w_mod, b_mod)

    as_f32 = lambda tables: tuple(jnp.asarray(t, F32) for t in tables)
    cos, s1, s2 = _rope_tables(sl, HEAD_DIM, 0, HEAD_DIM)
    gqa_tables = as_f32((cos[:, SPLIT_LANES], (s1 + s2)[:, SPLIT_LANES]))
    mla_q_tables = as_f32(_rope_tables(sl, MLA_ROPE, MLA_NOPE, 0))
    mla_k_tables = as_f32(_rope_tables(sl, MLA_ROPE, 0, 0))

    wr_t = jnp.zeros((LANES, D_MODEL), F32).at[:N_EXPERTS].set(w_router.T)
    wr_hi = wr_t.astype(BF16)
    wr_lo = (wr_t - wr_hi.astype(F32)).astype(BF16)
    rbias = jnp.zeros((LANES, 1), F32).at[:N_EXPERTS, 0].set(router_bias)
    wgu, wdn = _expert_weights(moe_w_gate, moe_w_up, moe_w_down)

    xc = x_prompt.reshape(nb * sc, D_MODEL)
    xl = x_sample.reshape(nd * sl, D_MODEL)
    tm_c, tm_l = PROJ_TOKENS, 2 * PROJ_TOKENS
    states = {k: [] for k in ("swa_k", "swa_v", "qkn_k", "qkn_v", "ckv", "kpe")}
    for i in range(DEPTH):
        kind, j = i % N_MIXERS, i // N_MIXERS
        m = mods[i]
        if kind in (0, 1):
            if kind == 0:
                w, wo, qn, kn = swa_wqkv[j], swa_wo[j], None, None
                ck, cv, sink = cache_swa_k[:, j], cache_swa_v[:, j], swa_sink[j]
            else:
                w, wo, qn, kn = qkn_wqkv[j], qkn_wo[j], qkn_qnorm[j], qkn_knorm[j]
                ck, cv, sink = cache_qkn_k[:, j], cache_qkn_v[:, j], None
            w = w.astype(BF16)
            qc, kc, vc, k32, v32 = _proj_gqa(xc, m, norm_mix[i], w, qn, kn, None, latent=False, tm=tm_c)
            ql, kl, vl = _proj_gqa(xl, m, norm_mix[i], w, qn, kn, gqa_tables, latent=True, tm=tm_l)
            wide = N_HEADS * HEAD_DIM
            oc = _attention(qc.reshape(nb, sc, wide), [(kc.reshape(nb, sc, wide), vc.reshape(nb, sc, wide))],
                            gqa=True, steps=1, tq=sc, sink=sink)
            ol = _attention(ql.reshape(nd, sl, wide),
                            [(kl.reshape(nd, sl, wide), vl.reshape(nd, sl, wide)),
                             (_cache_head_layout(ck, 0.0, split=True), _cache_head_layout(cv, 1.0))],
                            gqa=True, steps=1, tq=ATTN_Q_BAND if kind == 0 else ATTN_Q,
                            band=(kind == 0), sink=sink)
            names = ("swa_k", "swa_v") if kind == 0 else ("qkn_k", "qkn_v")
            states[names[0]].append(k32.reshape(nb, sc, N_KV_HEADS, HEAD_DIM))
            states[names[1]].append(v32.reshape(nb, sc, N_KV_HEADS, HEAD_DIM))
        else:
            wd, wuq, wk, wv = _mla_weights(mla_wdq[j], mla_wdkv[j], mla_wuq[j], mla_wukv[j])
            wo = mla_wo[j]
            qc, ckc, ckv32, kpe32 = _proj_mla(xc, m, norm_mix[i], wd, mla_qnorm[j], wuq, mla_kvnorm[j],
                                              None, None, latent=False, tm=tm_c)
            ql, ckl = _proj_mla(xl, m, norm_mix[i], wd, mla_qnorm[j], wuq, mla_kvnorm[j],
                                mla_q_tables, mla_k_tables, latent=True, tm=tm_c)
            cache = jnp.concatenate(
                [cache_mla_ckv[:, j], cache_mla_kpe[:, j], jnp.ones((nd, past, 1), F32),
                 jnp.zeros((nd, past, MLA_CK - MLA_KV_LORA - MLA_ROPE - 1), F32)], axis=-1).astype(BF16)
            ck_all = jnp.concatenate([ckl.reshape(nd, sl, MLA_CK), cache], axis=1)
            wq = MLA_HEADS * LANES
            kc, vc = _kv_expand(ckc, wk, wv, tm_c)
            kl, vl = _kv_expand(ck_all.reshape(nd * (sl + past), MLA_CK), wk, wv, 512)
            oc = _attention(qc.reshape(nb, sc, wq), [(kc.reshape(nb, sc, wq), vc.reshape(nb, sc, wq))],
                            gqa=False, steps=1, tq=sc)
            ol = _attention(ql.reshape(nd, sl, wq), [(kl.reshape(nd, sl + past, wq), vl.reshape(nd, sl + past, wq))],
                            gqa=False, steps=MLA_HEADS // 8, tq=ATTN_Q_MLA)
            states["ckv"].append(ckv32.reshape(nb, sc, MLA_KV_LORA))
            states["kpe"].append(kpe32.reshape(nb, sc, MLA_ROPE))
        final = i == DEPTH - 1
        pre = (m, norm_ffn[i], wo.astype(BF16), wr_hi, wr_lo, rbias)
        post = (m, norm_ffn[i], wgu, wdn, final_norm)
        xc, slots, wts, meta = _moe_pre(xc, oc.reshape(nb * sc, D_MODEL), *pre, latent=False)
        xc = _moe(xc, slots, wts, meta, *post, layer=i, latent=False, final=final)
        xl, slots, wts, meta = _moe_pre(xl, ol.reshape(nd * sl, D_MODEL), *pre, latent=True)
        xl = _moe(xl, slots, wts, meta, *post, layer=i, latent=True, final=final)
    return (xc.reshape(nb, sc, D_MODEL), xl.reshape(nd, sl, D_MODEL),
            jnp.stack(states["swa_k"], axis=1), jnp.stack(states["swa_v"], axis=1),
            jnp.stack(states["qkn_k"], axis=1), jnp.stack(states["qkn_v"], axis=1),
            jnp.stack(states["ckv"], axis=1), jnp.stack(states["kpe"], axis=1))
```
